```python
import math
import jax, jax.numpy as jnp
from jax import lax
import numpy as np

D_MODEL = 2048
BATCH = 2
SEQ = 4096
DEPTH = 2
DEC_BATCH = 4
DEC_SEQ = 4096
PAST_LEN = 128

MIX_W = D_MODEL
M_W = D_MODEL // 2
M_HEADS = 8
M_HEAD_DIM = M_W // M_HEADS
A_W = MIX_W - M_W
A_HEADS = 4
A_HEAD_DIM = A_W // (2 * A_HEADS)
A_QK_W = A_HEADS * 2 * A_HEAD_DIM
N_GATES = 4 * M_HEADS
D_FF = 5504
CONV_W = 3
CHUNK = 128
Q_BLOCK = 128
ROPE_THETA = 10000.0
EPS = 1e-6

OFF_QK_M = 0
OFF_V_M = OFF_QK_M + 2 * M_W
OFF_O_M = OFF_V_M + M_W
OFF_G_M = OFF_O_M + M_W
OFF_Q_A = OFF_G_M + N_GATES
OFF_K_A = OFF_Q_A + A_QK_W
OFF_V_A = OFF_K_A + A_QK_W
IN_COLS = OFF_V_A + A_W

kernel_name = "hybrid_mlstm_diffattn_macaron_encoder"


def _rms_norm(x, g):
    xf = x.astype(jnp.float32)
    y = xf * lax.rsqrt(jnp.mean(xf * xf, axis=-1, keepdims=True) + EPS)
    return (y * g.astype(jnp.float32)).astype(x.dtype)


def _swiglu(h, w_gate, w_up, w_down):
    return (jax.nn.silu(h @ w_gate) * (h @ w_up)) @ w_down


def _centred_conv(x, w):
    pad = CONV_W // 2
    S = x.shape[1]
    xp = jnp.pad(x, ((0, 0), (pad, pad), (0, 0)))
    y = xp[:, 0:S] * w[0]
    for j in range(1, CONV_W):
        y = y + xp[:, j:j + S] * w[j]
    return y


def _rope_tables(S, d):
    inv = 1.0 / (ROPE_THETA ** (jnp.arange(0, d, 2, dtype=jnp.float32) / d))
    ang = jnp.arange(S, dtype=jnp.float32)[:, None] * inv[None, :]
    emb = jnp.concatenate([ang, ang], axis=-1)
    return jnp.cos(emb), jnp.sin(emb)


def _apply_rope(x, cos, sin):
    c = cos[None, :, None, None, :]
    s = sin[None, :, None, None, :]
    x1, x2 = jnp.split(x, 2, axis=-1)
    rot = jnp.concatenate([-x2, x1], axis=-1)
    return x * c + rot * s


def _mlstm_chunkwise(q, k, v, ig, lf):
    B, S, H, d = q.shape
    nc = S // CHUNK
    def vec_chunks(a):
        return a.reshape(B, nc, CHUNK, H, d).transpose(1, 0, 3, 2, 4)
    def gate_chunks(a):
        return a.reshape(B, nc, CHUNK, H).transpose(1, 0, 3, 2)
    tril = jnp.tril(jnp.ones((CHUNK, CHUNK), dtype=bool))

    def step(carry, inp):
        C, n, m = carry
        qc, kc, vc, igc, lfc = inp
        b = jnp.cumsum(lfc, axis=-1)
        Dm = jnp.where(tril, b[..., :, None] - b[..., None, :] + igc[..., None, :], -jnp.inf)
        g = b + m[..., None]
        m_t = jnp.maximum(g, jnp.max(Dm, axis=-1))
        w_intra = jnp.exp(Dm - m_t[..., None])
        w_inter = jnp.exp(g - m_t)
        s = jnp.einsum('bhtd,bhsd->bhts', qc, kc) * w_intra
        num = jnp.einsum('bhts,bhsd->bhtd', s, vc) + w_inter[..., None] * jnp.einsum('bhvk,bhtk->bhtv', C, qc)
        den = jnp.sum(s, axis=-1) + w_inter * jnp.einsum('bhk,bhtk->bht', n, qc)
        h = num / jnp.maximum(jnp.abs(den), jnp.exp(-m_t))[..., None]
        bL = b[..., -1]
        a = bL[..., None] - b + igc
        m_new = jnp.maximum(bL + m, jnp.max(a, axis=-1))
        ws = jnp.exp(a - m_new[..., None])
        wc = jnp.exp(bL + m - m_new)
        C_new = wc[..., None, None] * C + jnp.einsum('bhs,bhsv,bhsk->bhvk', ws, vc, kc)
        n_new = wc[..., None] * n + jnp.einsum('bhs,bhsk->bhk', ws, kc)
        return (C_new, n_new, m_new), h

    init = (jnp.zeros((B, H, d, d), jnp.float32), jnp.zeros((B, H, d), jnp.float32), jnp.zeros((B, H), jnp.float32))
    _, hs = lax.scan(step, init, (vec_chunks(q), vec_chunks(k), vec_chunks(v), gate_chunks(ig), gate_chunks(lf)))
    return hs.transpose(1, 0, 3, 2, 4).reshape(B, S, H, d)


def _token_mix(h, w_in, conv_qk, b_gate, mlstm_g, lq1, lk1, lq2, lk2, subln_g, w_out, lambda_init):
    B, S, _ = h.shape
    f32 = jnp.float32
    z = h @ w_in
    qk_m, v_m, o_m, gates, q_a, k_a, v_a = jnp.split(z, [OFF_V_M, OFF_O_M, OFF_G_M, OFF_Q_A, OFF_K_A, OFF_V_A], axis=-1)

    qk = jax.nn.silu(_centred_conv(qk_m, conv_qk)).astype(f32)
    q_m, k_m = jnp.split(qk, 2, axis=-1)
    q_m = q_m.reshape(B, S, M_HEADS, M_HEAD_DIM)
    k_m = k_m.reshape(B, S, M_HEADS, M_HEAD_DIM) * (M_HEAD_DIM ** -0.5)
    v_m = v_m.astype(f32).reshape(B, S, M_HEADS, M_HEAD_DIM)
    gp = gates.astype(f32).reshape(B, S, 4, M_HEADS) + b_gate.astype(f32)[None, None]
    ig_f, lf_f = gp[:, :, 0], jax.nn.log_sigmoid(gp[:, :, 1])
    ig_b, lf_b = gp[:, :, 2], jax.nn.log_sigmoid(gp[:, :, 3])
    h_f = _mlstm_chunkwise(q_m, k_m, v_m, ig_f, lf_f)
    flip = lambda a: jnp.flip(a, axis=1)
    h_b = flip(_mlstm_chunkwise(flip(q_m), flip(k_m), flip(v_m), flip(ig_b), flip(lf_b)))
    hm = h_f + h_b
    hm = hm * lax.rsqrt(jnp.mean(hm * hm, axis=-1, keepdims=True) + EPS)
    hm = hm * mlstm_g.astype(f32).reshape(M_HEADS, M_HEAD_DIM)
    out_m = (jax.nn.sigmoid(o_m.astype(f32)) * hm.reshape(B, S, M_W))

    cos, sin = _rope_tables(S, A_HEAD_DIM)
    qa = _apply_rope(q_a.astype(f32).reshape(B, S, A_HEADS, 2, A_HEAD_DIM), cos, sin)
    ka = _apply_rope(k_a.astype(f32).reshape(B, S, A_HEADS, 2, A_HEAD_DIM), cos, sin)
    ka = ka.transpose(0, 2, 3, 1, 4)
    va = v_a.astype(f32).reshape(B, S, A_HEADS, 2 * A_HEAD_DIM).transpose(0, 2, 1, 3)
    lam = (jnp.exp(jnp.sum(lq1.astype(f32) * lk1.astype(f32))) - jnp.exp(jnp.sum(lq2.astype(f32) * lk2.astype(f32))) + lambda_init)
    scale = A_HEAD_DIM ** -0.5
    nq = S // Q_BLOCK
    qblocks = qa.transpose(0, 2, 3, 1, 4).reshape(B, A_HEADS, 2, nq, Q_BLOCK, A_HEAD_DIM).transpose(3, 0, 1, 2, 4, 5)

    def attend(qb):
        s = jnp.einsum('bhcqd,bhckd->bhcqk', qb, ka) * scale
        p = jax.nn.softmax(s, axis=-1)
        a = p[:, :, 0] - lam * p[:, :, 1]
        return jnp.einsum('bhqk,bhke->bhqe', a, va)

    oa = lax.map(attend, qblocks)
    oa = oa.transpose(1, 0, 3, 2, 4).reshape(B, S, A_HEADS, 2 * A_HEAD_DIM)
    oa = oa * lax.rsqrt(jnp.mean(oa * oa, axis=-1, keepdims=True) + EPS) * subln_g.astype(f32) * (1.0 - lambda_init)
    out_a = oa.reshape(B, S, A_W)

    mixed = jnp.concatenate([out_m, out_a], axis=-1).astype(h.dtype)
    return mixed @ w_out


def _trunk(x, p):
    for l in range(DEPTH):
        lambda_init = 0.8 - 0.6 * math.exp(-0.3 * l)
        x = x + 0.5 * _swiglu(_rms_norm(x, p['ffn1_norm'][l]), p['ffn1_w_gate'][l], p['ffn1_w_up'][l], p['ffn1_w_down'][l])
        x = x + _token_mix(_rms_norm(x, p['mix_norm'][l]), p['w_in'][l], p['conv_qk'][l], p['b_gate'][l], p['mlstm_norm'][l],
                           p['lambda_q1'][l], p['lambda_k1'][l], p['lambda_q2'][l], p['lambda_k2'][l], p['diff_subln'][l],
                           p['w_out'][l], lambda_init)
        x = x + 0.5 * _swiglu(_rms_norm(x, p['ffn2_norm'][l]), p['ffn2_w_gate'][l], p['ffn2_w_up'][l], p['ffn2_w_down'][l])
    return _rms_norm(x, p['final_norm'])


def setup_inputs(seed: int = 0) -> dict:
    key = jax.random.key(seed)
    ks = jax.random.split(key, 24)
    nrm = lambda k, shape, s: jax.random.normal(k, shape, jnp.float32) * s
    gain = lambda k, shape: 1.0 + 0.02 * jax.random.normal(k, shape, jnp.float32)
    f_bias = jnp.linspace(3.0, 6.0, M_HEADS, dtype=jnp.float32)
    bias_base = jnp.stack([jnp.zeros((M_HEADS,), jnp.float32), f_bias, jnp.zeros((M_HEADS,), jnp.float32), f_bias])
    return {
        'x_prompt': nrm(ks[0], (BATCH, SEQ, D_MODEL), 1.0),
        'x_sample': nrm(ks[1], (DEC_BATCH, DEC_SEQ, D_MODEL), 1.0),
        'ffn1_norm': gain(ks[2], (DEPTH, D_MODEL)),
        'ffn1_w_gate': nrm(ks[3], (DEPTH, D_MODEL, D_FF), D_MODEL ** -0.5),
        'ffn1_w_up': nrm(ks[4], (DEPTH, D_MODEL, D_FF), D_MODEL ** -0.5),
        'ffn1_w_down': nrm(ks[5], (DEPTH, D_FF, D_MODEL), D_FF ** -0.5),
        'mix_norm': gain(ks[6], (DEPTH, D_MODEL)),
        'w_in': nrm(ks[7], (DEPTH, D_MODEL, IN_COLS), D_MODEL ** -0.5),
        'conv_qk': nrm(ks[8], (DEPTH, CONV_W, 2 * M_W), CONV_W ** -0.5),
        'b_gate': bias_base[None] + nrm(ks[9], (DEPTH, 4, M_HEADS), 0.1),
        'mlstm_norm': gain(ks[10], (DEPTH, M_W)),
        'lambda_q1': nrm(ks[11], (DEPTH, A_HEAD_DIM), 0.1),
        'lambda_k1': nrm(ks[12], (DEPTH, A_HEAD_DIM), 0.1),
        'lambda_q2': nrm(ks[13], (DEPTH, A_HEAD_DIM), 0.1),
        'lambda_k2': nrm(ks[14], (DEPTH, A_HEAD_DIM), 0.1),
        'diff_subln': gain(ks[15], (DEPTH, 2 * A_HEAD_DIM)),
        'w_out': nrm(ks[16], (DEPTH, MIX_W, D_MODEL), MIX_W ** -0.5),
        'ffn2_norm': gain(ks[17], (DEPTH, D_MODEL)),
        'ffn2_w_gate': nrm(ks[18], (DEPTH, D_MODEL, D_FF), D_MODEL ** -0.5),
        'ffn2_w_up': nrm(ks[19], (DEPTH, D_MODEL, D_FF), D_MODEL ** -0.5),
        'ffn2_w_down': nrm(ks[20], (DEPTH, D_FF, D_MODEL), D_FF ** -0.5),
        'final_norm': gain(ks[21], (D_MODEL,)),
    }


def reference(x_prompt, x_sample, ffn1_norm, ffn1_w_gate, ffn1_w_up, ffn1_w_down, mix_norm, w_in, conv_qk, b_gate,
              mlstm_norm, lambda_q1, lambda_k1, lambda_q2, lambda_k2, diff_subln, w_out, ffn2_norm, ffn2_w_gate,
              ffn2_w_up, ffn2_w_down, final_norm):
    p = {
        'ffn1_norm': ffn1_norm, 'ffn1_w_gate': ffn1_w_gate, 'ffn1_w_up': ffn1_w_up, 'ffn1_w_down': ffn1_w_down,
        'mix_norm': mix_norm, 'w_in': w_in, 'conv_qk': conv_qk, 'b_gate': b_gate, 'mlstm_norm': mlstm_norm,
        'lambda_q1': lambda_q1, 'lambda_k1': lambda_k1, 'lambda_q2': lambda_q2, 'lambda_k2': lambda_k2,
        'diff_subln': diff_subln, 'w_out': w_out, 'ffn2_norm': ffn2_norm, 'ffn2_w_gate': ffn2_w_gate,
        'ffn2_w_up': ffn2_w_up, 'ffn2_w_down': ffn2_w_down, 'final_norm': final_norm,
    }
    y_prompt = _trunk(x_prompt, p)
    y_sample = _trunk(x_sample, p)
    return (y_prompt, y_sample)
```

```python
import functools
import math

import jax
import jax.numpy as jnp
from jax import lax
from jax.experimental import pallas as pl
from jax.experimental.pallas import tpu as pltpu

EPS = 1e-6
HEAD_DIM = 128
CHUNK = 128
CONV_W = 3
ROPE_THETA = 10000.0
N_GATE_KINDS = 4
LANES = 128
VMEM_LIMIT = 56 * 1024 * 1024

f32 = jnp.float32
bf16 = jnp.bfloat16


def _round_up(n, m):
    return (n + m - 1) // m * m


def _rms(x, g):
    ms = jnp.mean(x * x, axis=-1, keepdims=True)
    return x * lax.rsqrt(ms + EPS) * g


def _ffn_body(x_ref, g_ref, wg_ref, wu_ref, wd_ref, fg_ref, o_ref, xn_ref, *, final_norm):
    j = pl.program_id(1)

    @pl.when(j == 0)
    def _():
        x = x_ref[...]
        xn_ref[...] = _rms(x, g_ref[...]).astype(bf16)
        o_ref[...] = x

    xn = xn_ref[...]
    gate = jnp.dot(xn, wg_ref[...], preferred_element_type=f32)
    up = jnp.dot(xn, wu_ref[...], preferred_element_type=f32)
    h = (gate * jax.nn.sigmoid(gate) * up * 0.5).astype(bf16)
    o_ref[...] += jnp.dot(h, wd_ref[...], preferred_element_type=f32)

    if final_norm:
        @pl.when(j == pl.num_programs(1) - 1)
        def _():
            o_ref[...] = _rms(o_ref[...], fg_ref[...])


def _ffn(x, g, wg, wu, wd, fg, *, final_norm, tm, tf):
    T, D = x.shape
    Fp = wg.shape[1]
    return pl.pallas_call(
        functools.partial(_ffn_body, final_norm=final_norm),
        grid=(T // tm, Fp // tf),
        in_specs=[
            pl.BlockSpec((tm, D), lambda i, j: (i, 0)),
            pl.BlockSpec((1, D), lambda i, j: (0, 0)),
            pl.BlockSpec((D, tf), lambda i, j: (0, j)),
            pl.BlockSpec((D, tf), lambda i, j: (0, j)),
            pl.BlockSpec((tf, D), lambda i, j: (j, 0)),
            pl.BlockSpec((1, D), lambda i, j: (0, 0)),
        ],
        out_specs=pl.BlockSpec((tm, D), lambda i, j: (i, 0)),
        out_shape=jax.ShapeDtypeStruct((T, D), f32),
        scratch_shapes=[pltpu.VMEM((tm, D), bf16)],
        compiler_params=pltpu.CompilerParams(
            dimension_semantics=("parallel", "arbitrary"), vmem_limit_bytes=VMEM_LIMIT),
        name="ffn",
    )(x, g, wg, wu, wd, fg)


def _inproj_body(x_ref, g_ref, w_ref, wgate_ref, z_ref, gates_ref, xn_ref):
    j = pl.program_id(1)

    @pl.when(j == 0)
    def _():
        xn = _rms(x_ref[...], g_ref[...]).astype(bf16)
        xn_ref[...] = xn
        gates_ref[...] = jnp.dot(xn, wgate_ref[...], preferred_element_type=f32)

    z_ref[...] = jnp.dot(xn_ref[...], w_ref[...], preferred_element_type=f32)


def _inproj(x, g, w, wgate, *, tm, tn):
    T, D = x.shape
    N = w.shape[1]
    return pl.pallas_call(
        _inproj_body,
        grid=(T // tm, N // tn),
        in_specs=[
            pl.BlockSpec((tm, D), lambda i, j: (i, 0)),
            pl.BlockSpec((1, D), lambda i, j: (0, 0)),
            pl.BlockSpec((D, tn), lambda i, j: (0, j)),
            pl.BlockSpec((D, LANES), lambda i, j: (0, 0)),
        ],
        out_specs=[
            pl.BlockSpec((tm, tn), lambda i, j: (i, j)),
            pl.BlockSpec((tm, LANES), lambda i, j: (i, 0)),
        ],
        out_shape=[jax.ShapeDtypeStruct((T, N), f32), jax.ShapeDtypeStruct((T, LANES), f32)],
        scratch_shapes=[pltpu.VMEM((tm, D), bf16)],
        compiler_params=pltpu.CompilerParams(
            dimension_semantics=("parallel", "arbitrary"), vmem_limit_bytes=VMEM_LIMIT),
        name="inproj",
    )(x, g, w, wgate)


def _log_sigmoid(x):
    return jnp.minimum(x, 0.0) - jnp.log(1.0 + jnp.exp(-jnp.abs(x)))


def _mlstm_body(qp_ref, kp_ref, v_ref, og_ref, gt_ref, bias_ref, cwq_ref, cwk_ref, gain_ref, out_ref,
                q_s, kt_s, rows_s, cf_s, cb_s, hf_s, hb_s):
    S = qp_ref.shape[0]
    L = CHUNK
    nc = S // L
    d = HEAD_DIM

    def conv_silu(x, w):
        row = lax.broadcasted_iota(jnp.int32, x.shape, 0)
        prev = jnp.where(row == 0, 0.0, pltpu.roll(x, 1, 0))
        nxt = jnp.where(row == S - 1, 0.0, pltpu.roll(x, S - 1, 0))
        y = prev * w[0:1, :] + x * w[1:2, :] + nxt * w[2:3, :]
        return y * jax.nn.sigmoid(y)

    q_s[...] = conv_silu(qp_ref[...], cwq_ref[...]).astype(bf16)
    k_all = conv_silu(kp_ref[...], cwk_ref[...]) * (d ** -0.5)
    for c in range(nc):
        kt_s[c] = k_all[c * L:(c + 1) * L, :].T.astype(bf16)

    bias = bias_ref[...]
    ig_f = gt_ref[0] + bias[0:1, :]
    lf_f = _log_sigmoid(gt_ref[1] + bias[1:2, :])
    ig_b = gt_ref[2] + bias[2:3, :]
    lf_b = _log_sigmoid(gt_ref[3] + bias[3:4, :])
    r_i = lax.broadcasted_iota(jnp.int32, (L, L), 0)
    c_i = lax.broadcasted_iota(jnp.int32, (L, L), 1)
    lower = c_i <= r_i
    upper = c_i >= r_i
    eye = c_i == r_i
    b_f = jnp.dot(lf_f, upper.astype(f32), preferred_element_type=f32, precision=lax.Precision.HIGHEST)
    b_b = jnp.dot(lf_b, lower.astype(f32), preferred_element_type=f32, precision=lax.Precision.HIGHEST)
    rows_s[0] = b_f
    rows_s[1] = ig_f - b_f
    rows_s[2] = b_b
    rows_s[3] = ig_b - b_b

    cf_s[...] = jnp.zeros_like(cf_s)
    cb_s[...] = jnp.zeros_like(cb_s)
    ones_col = (lax.broadcasted_iota(jnp.int32, (L, LANES), 1) == 0).astype(f32)

    def one_direction(c, m, b_row, u_row, visible, full_row, c_s, h_s):
        r0 = pl.multiple_of(c * L, L)
        u_b = jnp.broadcast_to(u_row, (L, L))
        dm = jnp.where(visible, u_b, -jnp.inf)
        big_m = jnp.maximum(m, jnp.max(dm, axis=1, keepdims=True))
        w = jnp.exp(dm - big_m)
        b_col = jnp.sum(jnp.where(eye, jnp.broadcast_to(b_row, (L, L)), 0.0), axis=1, keepdims=True)
        u_col = jnp.sum(jnp.where(eye, u_b, 0.0), axis=1, keepdims=True)
        floor = jnp.exp(-(b_col + big_m))
        w_inter = jnp.exp(m - big_m)
        m_full = big_m[full_row:full_row + 1, :]
        ws = jnp.exp(u_col - m_full)
        wc = jnp.exp(m - m_full)
        m_new = b_row[:, full_row:full_row + 1] + m_full

        q = q_s[pl.ds(r0, L), :]
        kt = kt_s[c]
        v = v_ref[pl.ds(r0, L), :]
        v_aug = jnp.concatenate([v, ones_col], axis=1)
        s = jnp.dot(q, kt, preferred_element_type=f32) * w
        tot = jnp.dot(s.astype(bf16), v_aug.astype(bf16), preferred_element_type=f32)
        tot = tot + w_inter * jnp.dot(q, c_s[...].astype(bf16), preferred_element_type=f32)
        num = tot[:, :d]
        den = tot[:, d:d + 1]
        h_s[pl.ds(r0, L), :] = num / jnp.maximum(jnp.abs(den), floor)
        upd = jnp.dot(kt, (ws * v_aug).astype(bf16), preferred_element_type=f32)
        c_s[...] = wc * c_s[...] + upd
        return m_new

    def step(t, carry):
        m_f, m_b = carry
        cb = nc - 1 - t
        m_f = one_direction(t, m_f, rows_s[0, pl.ds(t, 1), :], rows_s[1, pl.ds(t, 1), :],
                            lower, L - 1, cf_s, hf_s)
        m_b = one_direction(cb, m_b, rows_s[2, pl.ds(cb, 1), :], rows_s[3, pl.ds(cb, 1), :],
                            upper, 0, cb_s, hb_s)
        return m_f, m_b

    zero = jnp.zeros((1, 1), f32)
    lax.fori_loop(0, nc, step, (zero, zero))

    hm = hf_s[...] + hb_s[...]
    hm = _rms(hm, gain_ref[...])
    out_ref[...] = (jax.nn.sigmoid(og_ref[...]) * hm).astype(out_ref.dtype)


def _mlstm(z, gt, bias, conv_w, gain, *, n_seq, S, n_heads):
    T = z.shape[0]
    d = HEAD_DIM
    nc = S // CHUNK
    H = n_heads
    seq_col = lambda off: pl.BlockSpec((S, d), lambda b, h: (b, off + h))
    return pl.pallas_call(
        _mlstm_body,
        grid=(n_seq, H),
        in_specs=[
            seq_col(0), seq_col(H), seq_col(2 * H), seq_col(3 * H),
            pl.BlockSpec((None, None, N_GATE_KINDS, nc, CHUNK), lambda b, h: (b, h, 0, 0, 0)),
            pl.BlockSpec((None, N_GATE_KINDS, LANES), lambda b, h: (h, 0, 0)),
            pl.BlockSpec((CONV_W, d), lambda b, h: (0, h)),
            pl.BlockSpec((CONV_W, d), lambda b, h: (0, H + h)),
            pl.BlockSpec((1, d), lambda b, h: (0, h)),
        ],
        out_specs=pl.BlockSpec((S, d), lambda b, h: (b, h)),
        out_shape=jax.ShapeDtypeStruct((T, H * d), bf16),
        scratch_shapes=[
            pltpu.VMEM((S, d), bf16),
            pltpu.VMEM((nc, d, CHUNK), bf16),
            pltpu.VMEM((N_GATE_KINDS, nc, CHUNK), f32),
            pltpu.VMEM((d, 2 * LANES), f32),
            pltpu.VMEM((d, 2 * LANES), f32),
            pltpu.VMEM((S, d), f32),
            pltpu.VMEM((S, d), f32),
        ],
        compiler_params=pltpu.CompilerParams(
            dimension_semantics=("parallel", "parallel"), vmem_limit_bytes=VMEM_LIMIT),
        name="mlstm",
    )(z, z, z, z, gt, bias, conv_w, conv_w, gain)


def _rope(x, cos, sin_signed):
    return x * cos + pltpu.roll(x, HEAD_DIM // 2, 1) * sin_signed


def _attn_body(q_ref, k_ref, v_ref, cosq_ref, sinq_ref, cos_ref, sin_ref, lq1_ref, lk1_ref, lq2_ref, lk2_ref,
               g_ref, out_ref, k1_s, k2_s, v_s, *, lambda_init):
    d = HEAD_DIM

    @pl.when(pl.program_id(2) == 0)
    def _():
        cos = cos_ref[...]
        sin = sin_ref[...]
        k1_s[...] = _rope(k_ref[:, :d], cos, sin).astype(bf16)
        k2_s[...] = _rope(k_ref[:, d:], cos, sin).astype(bf16)
        v_s[...] = v_ref[...].astype(bf16)

    lam = (jnp.exp(jnp.sum(lq1_ref[...] * lk1_ref[...], axis=1, keepdims=True))
           - jnp.exp(jnp.sum(lq2_ref[...] * lk2_ref[...], axis=1, keepdims=True)) + lambda_init)
    scale = d ** -0.5
    cos = cosq_ref[...]
    sin = sinq_ref[...]
    nt = (((1,), (1,)), ((), ()))

    def probs(qc, k_s):
        q = (_rope(qc, cos, sin) * scale).astype(bf16)
        s = lax.dot_general(q, k_s[...], nt, preferred_element_type=f32)
        p = jnp.exp(s - jnp.max(s, axis=1, keepdims=True))
        return p, jnp.sum(p, axis=1, keepdims=True)

    p1, l1 = probs(q_ref[:, :d], k1_s)
    p2, l2 = probs(q_ref[:, d:], k2_s)
    a = p1 * (1.0 / l1) - p2 * (lam / l2)
    o = jnp.dot(a.astype(bf16), v_s[...], preferred_element_type=f32)
    o = _rms(o, g_ref[...]) * (1.0 - lambda_init)
    out_ref[...] = o.astype(out_ref.dtype)


def _attn(z, cos, sin_signed, lq1, lk1, lq2, lk2, g, *, n_seq, S, n_heads, col0, lambda_init, tq):
    T = z.shape[0]
    d = HEAD_DIM
    nq = S // tq
    A = n_heads
    row = lambda b, h, i: (0, 0)
    return pl.pallas_call(
        functools.partial(_attn_body, lambda_init=lambda_init),
        grid=(n_seq, A, nq),
        in_specs=[
            pl.BlockSpec((tq, 2 * d), lambda b, h, i: (b * nq + i, col0 + h)),
            pl.BlockSpec((S, 2 * d), lambda b, h, i: (b, col0 + A + h)),
            pl.BlockSpec((S, 2 * d), lambda b, h, i: (b, col0 + 2 * A + h)),
            pl.BlockSpec((tq, d), lambda b, h, i: (i, 0)),
            pl.BlockSpec((tq, d), lambda b, h, i: (i, 0)),
            pl.BlockSpec((S, d), row),
            pl.BlockSpec((S, d), row),
            pl.BlockSpec((1, d), row), pl.BlockSpec((1, d), row),
            pl.BlockSpec((1, d), row), pl.BlockSpec((1, d), row),
            pl.BlockSpec((1, 2 * d), row),
        ],
        out_specs=pl.BlockSpec((tq, 2 * d), lambda b, h, i: (b * nq + i, h)),
        out_shape=jax.ShapeDtypeStruct((T, A * 2 * d), bf16),
        scratch_shapes=[pltpu.VMEM((S, d), bf16), pltpu.VMEM((S, d), bf16), pltpu.VMEM((S, 2 * d), bf16)],
        compiler_params=pltpu.CompilerParams(
            dimension_semantics=("parallel", "parallel", "arbitrary"), vmem_limit_bytes=VMEM_LIMIT),
        name="diffattn",
    )(z, z, z, cos, sin_signed, cos, sin_signed, lq1, lk1, lq2, lk2, g)


def _outproj_body(x_ref, om_ref, oa_ref, wm_ref, wa_ref, o_ref):
    o_ref[...] = (x_ref[...] + jnp.dot(om_ref[...], wm_ref[...], preferred_element_type=f32)
                  + jnp.dot(oa_ref[...], wa_ref[...], preferred_element_type=f32))


def _outproj(x, om, oa, wm, wa, *, tm):
    T, D = x.shape
    Wm, Wa = om.shape[1], oa.shape[1]
    return pl.pallas_call(
        _outproj_body,
        grid=(T // tm,),
        in_specs=[
            pl.BlockSpec((tm, D), lambda i: (i, 0)),
            pl.BlockSpec((tm, Wm), lambda i: (i, 0)),
            pl.BlockSpec((tm, Wa), lambda i: (i, 0)),
            pl.BlockSpec((Wm, D), lambda i: (0, 0)),
            pl.BlockSpec((Wa, D), lambda i: (0, 0)),
        ],
        out_specs=pl.BlockSpec((tm, D), lambda i: (i, 0)),
        out_shape=jax.ShapeDtypeStruct((T, D), f32),
        compiler_params=pltpu.CompilerParams(
            dimension_semantics=("parallel",), vmem_limit_bytes=VMEM_LIMIT),
        name="outproj",
    )(x, om, oa, wm, wa)


def _tiles(T, S):
    tm = math.gcd(512, T)
    tq = math.gcd(256, S)
    return tm, tq


def kernel(x_prompt, x_sample, ffn1_norm, ffn1_w_gate, ffn1_w_up, ffn1_w_down, mix_norm, w_in, conv_qk, b_gate,
           mlstm_norm, lambda_q1, lambda_k1, lambda_q2, lambda_k2, diff_subln, w_out, ffn2_norm, ffn2_w_gate,
           ffn2_w_up, ffn2_w_down, final_norm):
    depth, D, F = ffn1_w_gate.shape
    S = x_prompt.shape[1]
    assert x_sample.shape[1] == S and S % CHUNK == 0
    d = HEAD_DIM
    m_w = D // 2
    MH = m_w // d
    a_w = D - m_w
    AH = a_w // (2 * d)
    n_gates = N_GATE_KINDS * MH
    assert n_gates <= LANES and w_in.shape[2] == 4 * m_w + n_gates + 3 * a_w
    off_g = 4 * m_w
    n_seq = x_prompt.shape[0] + x_sample.shape[0]
    T = n_seq * S
    tm, tq = _tiles(T, S)
    tf = min(512, _round_up(F, LANES))
    Fp = _round_up(F, tf)
    n_main = 4 * m_w + 3 * a_w
    tn = 1024 if n_main % 1024 == 0 else 256
    nc = S // CHUNK

    x = jnp.concatenate([x_prompt.reshape(-1, D), x_sample.reshape(-1, D)], axis=0)

    inv = 1.0 / (ROPE_THETA ** (jnp.arange(0, d, 2, dtype=f32) / d))
    ang = jnp.arange(S, dtype=f32)[:, None] * inv[None, :]
    emb = jnp.concatenate([ang, ang], axis=-1)
    cos = jnp.cos(emb)
    sin_signed = jnp.sin(emb) * jnp.concatenate([-jnp.ones((d // 2,), f32), jnp.ones((d // 2,), f32)])

    def pad_ff(w, axis):
        pad = [(0, 0), (0, 0)]
        pad[axis] = (0, Fp - F)
        return jnp.pad(w, pad).astype(bf16)

    row = lambda v: v.reshape(1, -1).astype(f32)

    for l in range(depth):
        lambda_init = 0.8 - 0.6 * math.exp(-0.3 * l)
        x = _ffn(x, row(ffn1_norm[l]), pad_ff(ffn1_w_gate[l], 1), pad_ff(ffn1_w_up[l], 1), pad_ff(ffn1_w_down[l], 0),
                 row(final_norm), final_norm=False, tm=tm, tf=tf)

        w = w_in[l]
        w_main = jnp.concatenate([w[:, :off_g], w[:, off_g + n_gates:]], axis=1).astype(bf16)
        w_gate = w[:, off_g:off_g + n_gates].reshape(D, N_GATE_KINDS, MH).transpose(0, 2, 1).reshape(D, n_gates)
        w_gate = jnp.pad(w_gate, ((0, 0), (0, LANES - n_gates))).astype(bf16)
        z, gates = _inproj(x, row(mix_norm[l]), w_main, w_gate, tm=tm, tn=tn)

        gt = gates[:, :n_gates].reshape(n_seq, nc, CHUNK, MH, N_GATE_KINDS).transpose(0, 3, 4, 1, 2)
        bias = jnp.broadcast_to(b_gate[l].astype(f32).T[:, :, None], (MH, N_GATE_KINDS, LANES))
        om = _mlstm(z, gt, bias, conv_qk[l].astype(f32), row(mlstm_norm[l]), n_seq=n_seq, S=S, n_heads=MH)
        oa = _attn(z, cos, sin_signed, row(lambda_q1[l]), row(lambda_k1[l]), row(lambda_q2[l]), row(lambda_k2[l]),
                   row(diff_subln[l]), n_seq=n_seq, S=S, n_heads=AH, col0=4 * m_w // (2 * d),
                   lambda_init=lambda_init, tq=tq)
        wo = w_out[l].astype(bf16)
        x = _outproj(x, om, oa, wo[:m_w], wo[m_w:], tm=tm)

        x = _ffn(x, row(ffn2_norm[l]), pad_ff(ffn2_w_gate[l], 1), pad_ff(ffn2_w_up[l], 1), pad_ff(ffn2_w_down[l], 0),
                 row(final_norm), final_norm=(l == depth - 1), tm=tm, tf=tf)

    n_p = x_prompt.shape[0] * S
    return x[:n_p].reshape(x_prompt.shape), x[n_p:].reshape(x_sample.shape)
```

```python
import functools
import math

import jax
import jax.numpy as jnp
from jax import lax
from jax.experimental import pallas as pl
from jax.experimental.pallas import tpu as pltpu

EPS = 1e-6
HEAD_DIM = 128
CHUNK = 128
CONV_W = 3
ROPE_THETA = 10000.0
N_GATE_KINDS = 4
LANES = 128
VMEM_LIMIT = 56 * 1024 * 1024

f32 = jnp.float32
bf16 = jnp.bfloat16


def _round_up(n, m):
    return (n + m - 1) // m * m


def _rms(x, g):
    ms = jnp.mean(x * x, axis=-1, keepdims=True)
    return x * lax.rsqrt(ms + EPS) * g


def _ffn_body(x_ref, g_ref, wg_ref, wu_ref, wd_ref, fg_ref, o_ref, xn_ref, *, final_norm):
    j = pl.program_id(1)

    @pl.when(j == 0)
    def _():
        x = x_ref[...]
        xn_ref[...] = _rms(x, g_ref[...]).astype(bf16)
        o_ref[...] = x

    xn = xn_ref[...]
    gate = jnp.dot(xn, wg_ref[...], preferred_element_type=f32)
    up = jnp.dot(xn, wu_ref[...], preferred_element_type=f32)
    h = (gate * jax.nn.sigmoid(gate) * up * 0.5).astype(bf16)
    o_ref[...] += jnp.dot(h, wd_ref[...], preferred_element_type=f32)

    if final_norm:
        @pl.when(j == pl.num_programs(1) - 1)
        def _():
            o_ref[...] = _rms(o_ref[...], fg_ref[...])


def _ffn(x, g, wg, wu, wd, fg, *, final_norm, tm, tf):
    T, D = x.shape
    Fp = wg.shape[1]
    return pl.pallas_call(
        functools.partial(_ffn_body, final_norm=final_norm),
        grid=(T // tm, Fp // tf),
        in_specs=[
            pl.BlockSpec((tm, D), lambda i, j: (i, 0)),
            pl.BlockSpec((1, D), lambda i, j: (0, 0)),
            pl.BlockSpec((D, tf), lambda i, j: (0, j)),
            pl.BlockSpec((D, tf), lambda i, j: (0, j)),
            pl.BlockSpec((tf, D), lambda i, j: (j, 0)),
            pl.BlockSpec((1, D), lambda i, j: (0, 0)),
        ],
        out_specs=pl.BlockSpec((tm, D), lambda i, j: (i, 0)),
        out_shape=jax.ShapeDtypeStruct((T, D), f32),
        scratch_shapes=[pltpu.VMEM((tm, D), bf16)],
        compiler_params=pltpu.CompilerParams(
            dimension_semantics=("parallel", "arbitrary"), vmem_limit_bytes=VMEM_LIMIT),
        name="ffn",
    )(x, g, wg, wu, wd, fg)


def _rope(x, cos, sin_signed):
    return x * cos + pltpu.roll(x, HEAD_DIM // 2, 1) * sin_signed


def _inproj_body(x_ref, g_ref, w_ref, wgate_ref, cos_ref, sin_ref, z_ref, gates_ref, xn_ref, *, q_tiles, k_tiles,
                 q_scale):
    j = pl.program_id(1)

    @pl.when(j == 0)
    def _():
        xn = _rms(x_ref[...], g_ref[...]).astype(bf16)
        xn_ref[...] = xn
        gates_ref[...] = jnp.dot(xn, wgate_ref[...], preferred_element_type=f32)

    acc = jnp.dot(xn_ref[...], w_ref[...], preferred_element_type=f32)
    is_q = (j >= q_tiles[0]) & (j < q_tiles[1])
    is_k = (j >= k_tiles[0]) & (j < k_tiles[1])

    @pl.when(is_q | is_k)
    def _():
        sc = jnp.where(is_q, q_scale, 1.0)
        cos = cos_ref[...] * sc
        sin = sin_ref[...] * sc
        for c in range(acc.shape[1] // HEAD_DIM):
            cols = slice(c * HEAD_DIM, (c + 1) * HEAD_DIM)
            z_ref[:, cols] = _rope(acc[:, cols], cos, sin).astype(z_ref.dtype)

    @pl.when(jnp.logical_not(is_q | is_k))
    def _():
        z_ref[...] = acc.astype(z_ref.dtype)


def _inproj(x, g, w, wgate, cos, sin_signed, *, tm, tn, S, q_cols, k_cols, q_scale):
    T, D = x.shape
    N = w.shape[1]
    n_pos = S // tm
    tiles = lambda cols: (cols[0] // tn, cols[1] // tn)
    assert all(c % tn == 0 for c in q_cols + k_cols) and S % tm == 0
    return pl.pallas_call(
        functools.partial(_inproj_body, q_tiles=tiles(q_cols), k_tiles=tiles(k_cols), q_scale=q_scale),
        grid=(T // tm, N // tn),
        in_specs=[
            pl.BlockSpec((tm, D), lambda i, j: (i, 0)),
            pl.BlockSpec((1, D), lambda i, j: (0, 0)),
            pl.BlockSpec((D, tn), lambda i, j: (0, j)),
            pl.BlockSpec((D, LANES), lambda i, j: (0, 0)),
            pl.BlockSpec((tm, HEAD_DIM), lambda i, j: (i % n_pos, 0)),
            pl.BlockSpec((tm, HEAD_DIM), lambda i, j: (i % n_pos, 0)),
        ],
        out_specs=[
            pl.BlockSpec((tm, tn), lambda i, j: (i, j)),
            pl.BlockSpec((tm, LANES), lambda i, j: (i, 0)),
        ],
        out_shape=[jax.ShapeDtypeStruct((T, N), bf16), jax.ShapeDtypeStruct((T, LANES), f32)],
        scratch_shapes=[pltpu.VMEM((tm, D), bf16)],
        compiler_params=pltpu.CompilerParams(
            dimension_semantics=("parallel", "arbitrary"), vmem_limit_bytes=VMEM_LIMIT),
        name="inproj",
    )(x, g, w, wgate, cos, sin_signed)


def _log_sigmoid(x):
    return jnp.minimum(x, 0.0) - jnp.log(1.0 + jnp.exp(-jnp.abs(x)))


def _mlstm_body(qp_ref, kp_ref, v_ref, og_ref, gt_ref, bias_ref, cwq_ref, cwk_ref, gain_ref, out_ref,
                q_s, kt_s, rows_s, cf_s, cb_s, hf_s, hb_s):
    S = qp_ref.shape[0]
    L = CHUNK
    nc = S // L
    d = HEAD_DIM

    def conv_silu(x, w):
        row = lax.broadcasted_iota(jnp.int32, x.shape, 0)
        prev = jnp.where(row == 0, 0.0, pltpu.roll(x, 1, 0))
        nxt = jnp.where(row == S - 1, 0.0, pltpu.roll(x, S - 1, 0))
        y = prev * w[0:1, :] + x * w[1:2, :] + nxt * w[2:3, :]
        return y * jax.nn.sigmoid(y)

    q_s[...] = conv_silu(qp_ref[...].astype(f32), cwq_ref[...]).astype(bf16)
    k_all = conv_silu(kp_ref[...].astype(f32), cwk_ref[...]) * (d ** -0.5)
    for c in range(nc):
        kt_s[c] = k_all[c * L:(c + 1) * L, :].T.astype(bf16)

    bias = bias_ref[...]
    ig_f = gt_ref[0] + bias[0:1, :]
    lf_f = _log_sigmoid(gt_ref[1] + bias[1:2, :])
    ig_b = gt_ref[2] + bias[2:3, :]
    lf_b = _log_sigmoid(gt_ref[3] + bias[3:4, :])
    r_i = lax.broadcasted_iota(jnp.int32, (L, L), 0)
    c_i = lax.broadcasted_iota(jnp.int32, (L, L), 1)
    lower = c_i <= r_i
    upper = c_i >= r_i
    eye = c_i == r_i
    b_f = jnp.dot(lf_f, upper.astype(f32), preferred_element_type=f32, precision=lax.Precision.HIGHEST)
    b_b = jnp.dot(lf_b, lower.astype(f32), preferred_element_type=f32, precision=lax.Precision.HIGHEST)
    rows_s[0] = b_f
    rows_s[1] = ig_f - b_f
    rows_s[2] = b_b
    rows_s[3] = ig_b - b_b

    cf_s[...] = jnp.zeros_like(cf_s)
    cb_s[...] = jnp.zeros_like(cb_s)
    ones_col = (lax.broadcasted_iota(jnp.int32, (L, LANES), 1) == 0).astype(bf16)

    def one_direction(c, m, b_row, u_row, visible, full_row, c_s, h_s):
        r0 = pl.multiple_of(c * L, L)
        u_b = jnp.broadcast_to(u_row, (L, L))
        dm = jnp.where(visible, u_b, -jnp.inf)
        big_m = jnp.maximum(m, jnp.max(dm, axis=1, keepdims=True))
        w = jnp.exp(dm - big_m)
        b_col = jnp.sum(jnp.where(eye, jnp.broadcast_to(b_row, (L, L)), 0.0), axis=1, keepdims=True)
        u_col = jnp.sum(jnp.where(eye, u_b, 0.0), axis=1, keepdims=True)
        floor = jnp.exp(-(b_col + big_m))
        w_inter = jnp.exp(m - big_m)
        m_full = big_m[full_row:full_row + 1, :]
        ws = jnp.exp(u_col - m_full)
        wc = jnp.exp(m - m_full)
        m_new = b_row[:, full_row:full_row + 1] + m_full

        q = q_s[pl.ds(r0, L), :]
        kt = kt_s[c]
        v = v_ref[pl.ds(r0, L), :]
        v_aug = jnp.concatenate([v, ones_col], axis=1)
        s = jnp.dot(q, kt, preferred_element_type=f32) * w
        tot = jnp.dot(s.astype(bf16), v_aug, preferred_element_type=f32)
        tot = tot + w_inter * jnp.dot(q, c_s[...].astype(bf16), preferred_element_type=f32)
        num = tot[:, :d]
        den = tot[:, d:d + 1]
        h_s[pl.ds(r0, L), :] = num / jnp.maximum(jnp.abs(den), floor)
        upd = jnp.dot(kt, (ws * v_aug.astype(f32)).astype(bf16), preferred_element_type=f32)
        c_s[...] = wc * c_s[...] + upd
        return m_new

    def step(t, carry):
        m_f, m_b = carry
        cb = nc - 1 - t
        m_f = one_direction(t, m_f, rows_s[0, pl.ds(t, 1), :], rows_s[1, pl.ds(t, 1), :],
                            lower, L - 1, cf_s, hf_s)
        m_b = one_direction(cb, m_b, rows_s[2, pl.ds(cb, 1), :], rows_s[3, pl.ds(cb, 1), :],
                            upper, 0, cb_s, hb_s)
        return m_f, m_b

    zero = jnp.zeros((1, 1), f32)
    lax.fori_loop(0, nc, step, (zero, zero))

    hm = hf_s[...] + hb_s[...]
    hm = _rms(hm, gain_ref[...])
    out_ref[...] = (jax.nn.sigmoid(og_ref[...].astype(f32)) * hm).astype(out_ref.dtype)


def _mlstm(z, gt, bias, conv_w, gain, *, n_seq, S, n_heads):
    T = z.shape[0]
    d = HEAD_DIM
    nc = S // CHUNK
    H = n_heads
    seq_col = lambda off: pl.BlockSpec((S, d), lambda b, h: (b, off + h))
    return pl.pallas_call(
        _mlstm_body,
        grid=(n_seq, H),
        in_specs=[
            seq_col(0), seq_col(H), seq_col(2 * H), seq_col(3 * H),
            pl.BlockSpec((None, None, N_GATE_KINDS, nc, CHUNK), lambda b, h: (b, h, 0, 0, 0)),
            pl.BlockSpec((None, N_GATE_KINDS, LANES), lambda b, h: (h, 0, 0)),
            pl.BlockSpec((CONV_W, d), lambda b, h: (0, h)),
            pl.BlockSpec((CONV_W, d), lambda b, h: (0, H + h)),
            pl.BlockSpec((1, d), lambda b, h: (0, h)),
        ],
        out_specs=pl.BlockSpec((S, d), lambda b, h: (b, h)),
        out_shape=jax.ShapeDtypeStruct((T, H * d), bf16),
        scratch_shapes=[
            pltpu.VMEM((S, d), bf16),
            pltpu.VMEM((nc, d, CHUNK), bf16),
            pltpu.VMEM((N_GATE_KINDS, nc, CHUNK), f32),
            pltpu.VMEM((d, 2 * LANES), f32),
            pltpu.VMEM((d, 2 * LANES), f32),
            pltpu.VMEM((S, d), f32),
            pltpu.VMEM((S, d), f32),
        ],
        compiler_params=pltpu.CompilerParams(
            dimension_semantics=("parallel", "parallel"), vmem_limit_bytes=VMEM_LIMIT),
        name="mlstm",
    )(z, z, z, z, gt, bias, conv_w, conv_w, gain)


def _attn_body(q_ref, k_ref, v_ref, lq1_ref, lk1_ref, lq2_ref, lk2_ref, g_ref, out_ref, *, lambda_init, tk):
    d = HEAD_DIM
    S = k_ref.shape[0]
    nt = (((1,), (1,)), ((), ()))

    def component(q, col):
        m = l = acc = None
        for c in range(S // tk):
            rows = slice(c * tk, (c + 1) * tk)
            s = lax.dot_general(q, k_ref[rows, col:col + d], nt, preferred_element_type=f32)
            m_c = jnp.max(s, axis=1, keepdims=True)
            if c == 0:
                m = m_c
                p = jnp.exp2(s - m)
                l = jnp.sum(p, axis=1, keepdims=True)
                acc = jnp.dot(p.astype(bf16), v_ref[rows, :], preferred_element_type=f32)
            else:
                m_new = jnp.maximum(m, m_c)
                alpha = jnp.exp2(m - m_new)
                p = jnp.exp2(s - m_new)
                l = alpha * l + jnp.sum(p, axis=1, keepdims=True)
                acc = alpha * acc + jnp.dot(p.astype(bf16), v_ref[rows, :], preferred_element_type=f32)
                m = m_new
        return acc, l

    acc1, l1 = component(q_ref[:, :d], 0)
    acc2, l2 = component(q_ref[:, d:], d)
    lam = (jnp.exp(jnp.sum(lq1_ref[...] * lk1_ref[...], axis=1, keepdims=True))
           - jnp.exp(jnp.sum(lq2_ref[...] * lk2_ref[...], axis=1, keepdims=True)) + lambda_init)
    o = acc1 * (1.0 / l1) - acc2 * (lam / l2)
    o = _rms(o, g_ref[...]) * (1.0 - lambda_init)
    out_ref[...] = o.astype(out_ref.dtype)


def _attn(z, lq1, lk1, lq2, lk2, g, *, n_seq, S, n_heads, col0, lambda_init, tq, tk):
    T = z.shape[0]
    d = HEAD_DIM
    nq = S // tq
    A = n_heads
    row = lambda b, h, i: (0, 0)
    return pl.pallas_call(
        functools.partial(_attn_body, lambda_init=lambda_init, tk=tk),
        grid=(n_seq, A, nq),
        in_specs=[
            pl.BlockSpec((tq, 2 * d), lambda b, h, i: (b * nq + i, col0 + h)),
            pl.BlockSpec((S, 2 * d), lambda b, h, i: (b, col0 + A + h)),
            pl.BlockSpec((S, 2 * d), lambda b, h, i: (b, col0 + 2 * A + h)),
            pl.BlockSpec((1, d), row), pl.BlockSpec((1, d), row),
            pl.BlockSpec((1, d), row), pl.BlockSpec((1, d), row),
            pl.BlockSpec((1, 2 * d), row),
        ],
        out_specs=pl.BlockSpec((tq, 2 * d), lambda b, h, i: (b * nq + i, h)),
        out_shape=jax.ShapeDtypeStruct((T, A * 2 * d), bf16),
        compiler_params=pltpu.CompilerParams(
            dimension_semantics=("parallel", "parallel", "parallel"), vmem_limit_bytes=VMEM_LIMIT),
        name="diffattn",
    )(z, z, z, lq1, lk1, lq2, lk2, g)


def _outproj_body(x_ref, om_ref, oa_ref, wm_ref, wa_ref, o_ref):
    o_ref[...] = (x_ref[...] + jnp.dot(om_ref[...], wm_ref[...], preferred_element_type=f32)
                  + jnp.dot(oa_ref[...], wa_ref[...], preferred_element_type=f32))


def _outproj(x, om, oa, wm, wa, *, tm):
    T, D = x.shape
    Wm, Wa = om.shape[1], oa.shape[1]
    return pl.pallas_call(
        _outproj_body,
        grid=(T // tm,),
        in_specs=[
            pl.BlockSpec((tm, D), lambda i: (i, 0)),
            pl.BlockSpec((tm, Wm), lambda i: (i, 0)),
            pl.BlockSpec((tm, Wa), lambda i: (i, 0)),
            pl.BlockSpec((Wm, D), lambda i: (0, 0)),
            pl.BlockSpec((Wa, D), lambda i: (0, 0)),
        ],
        out_specs=pl.BlockSpec((tm, D), lambda i: (i, 0)),
        out_shape=jax.ShapeDtypeStruct((T, D), f32),
        compiler_params=pltpu.CompilerParams(
            dimension_semantics=("parallel",), vmem_limit_bytes=VMEM_LIMIT),
        name="outproj",
    )(x, om, oa, wm, wa)


def _tiles(T, S):
    tm = math.gcd(512, math.gcd(T, S))
    tq = math.gcd(512, S)
    return tm, tq


def kernel(x_prompt, x_sample, ffn1_norm, ffn1_w_gate, ffn1_w_up, ffn1_w_down, mix_norm, w_in, conv_qk, b_gate,
           mlstm_norm, lambda_q1, lambda_k1, lambda_q2, lambda_k2, diff_subln, w_out, ffn2_norm, ffn2_w_gate,
           ffn2_w_up, ffn2_w_down, final_norm):
    depth, D, F = ffn1_w_gate.shape
    S = x_prompt.shape[1]
    assert x_sample.shape[1] == S and S % CHUNK == 0
    d = HEAD_DIM
    m_w = D // 2
    MH = m_w // d
    a_w = D - m_w
    AH = a_w // (2 * d)
    n_gates = N_GATE_KINDS * MH
    assert n_gates <= LANES and w_in.shape[2] == 4 * m_w + n_gates + 3 * a_w
    off_g = 4 * m_w
    n_seq = x_prompt.shape[0] + x_sample.shape[0]
    T = n_seq * S
    tm, tq = _tiles(T, S)
    tf = min(512, _round_up(F, LANES))
    Fp = _round_up(F, tf)
    n_main = 4 * m_w + 3 * a_w
    tn = 1024 if n_main % 1024 == 0 else 256
    nc = S // CHUNK

    x = jnp.concatenate([x_prompt.reshape(-1, D), x_sample.reshape(-1, D)], axis=0)

    inv = 1.0 / (ROPE_THETA ** (jnp.arange(0, d, 2, dtype=f32) / d))
    ang = jnp.arange(S, dtype=f32)[:, None] * inv[None, :]
    emb = jnp.concatenate([ang, ang], axis=-1)
    cos = jnp.cos(emb)
    sin_signed = jnp.sin(emb) * jnp.concatenate([-jnp.ones((d // 2,), f32), jnp.ones((d // 2,), f32)])

    def pad_ff(w, axis):
        pad = [(0, 0), (0, 0)]
        pad[axis] = (0, Fp - F)
        return jnp.pad(w, pad).astype(bf16)

    row = lambda v: v.reshape(1, -1).astype(f32)

    for l in range(depth):
        lambda_init = 0.8 - 0.6 * math.exp(-0.3 * l)
        x = _ffn(x, row(ffn1_norm[l]), pad_ff(ffn1_w_gate[l], 1), pad_ff(ffn1_w_up[l], 1), pad_ff(ffn1_w_down[l], 0),
                 row(final_norm), final_norm=False, tm=tm, tf=tf)

        w = w_in[l]
        w_main = jnp.concatenate([w[:, :off_g], w[:, off_g + n_gates:]], axis=1).astype(bf16)
        w_gate = w[:, off_g:off_g + n_gates].reshape(D, N_GATE_KINDS, MH).transpose(0, 2, 1).reshape(D, n_gates)
        w_gate = jnp.pad(w_gate, ((0, 0), (0, LANES - n_gates))).astype(bf16)
        z, gates = _inproj(x, row(mix_norm[l]), w_main, w_gate, cos, sin_signed, tm=tm, tn=tn, S=S,
                           q_cols=(4 * m_w, 4 * m_w + a_w), k_cols=(4 * m_w + a_w, 4 * m_w + 2 * a_w),
                           q_scale=d ** -0.5 * math.log2(math.e))

        gt = gates[:, :n_gates].reshape(n_seq, nc, CHUNK, MH, N_GATE_KINDS).transpose(0, 3, 4, 1, 2)
        bias = jnp.broadcast_to(b_gate[l].astype(f32).T[:, :, None], (MH, N_GATE_KINDS, LANES))
        om = _mlstm(z, gt, bias, conv_qk[l].astype(f32), row(mlstm_norm[l]), n_seq=n_seq, S=S, n_heads=MH)
        oa = _attn(z, row(lambda_q1[l]), row(lambda_k1[l]), row(lambda_q2[l]), row(lambda_k2[l]),
                   row(diff_subln[l]), n_seq=n_seq, S=S, n_heads=AH, col0=4 * m_w // (2 * d),
                   lambda_init=lambda_init, tq=tq, tk=tq)
        wo = w_out[l].astype(bf16)
        x = _outproj(x, om, oa, wo[:m_w], wo[m_w:], tm=tm)

        x = _ffn(x, row(ffn2_norm[l]), pad_ff(ffn2_w_gate[l], 1), pad_ff(ffn2_w_up[l], 1), pad_ff(ffn2_w_down[l], 0),
                 row(final_norm), final_norm=(l == depth - 1), tm=tm, tf=tf)

    n_p = x_prompt.shape[0] * S
    return x[:n_p].reshape(x_prompt.shape), x[n_p:].reshape(x_sample.shape)
```

```python
import functools
import math

import jax
import jax.numpy as jnp
from jax import lax
from jax.experimental import pallas as pl
from jax.experimental.pallas import tpu as pltpu

EPS = 1e-6
HEAD_DIM = 128
CHUNK = 128
CONV_W = 3
ROPE_THETA = 10000.0
N_GATE_KINDS = 4
LANES = 128
VMEM_LIMIT = 56 * 1024 * 1024

f32 = jnp.float32
bf16 = jnp.bfloat16


def _round_up(n, m):
    return (n + m - 1) // m * m


def _rms(x, g):
    ms = jnp.mean(x * x, axis=-1, keepdims=True)
    return x * lax.rsqrt(ms + EPS) * g


def _ffn_body(x_ref, g_ref, wg_ref, wu_ref, wd_ref, fg_ref, o_ref, xn_ref, *, final_norm):
    j = pl.program_id(1)

    @pl.when(j == 0)
    def _():
        x = x_ref[...]
        xn_ref[...] = _rms(x, g_ref[...]).astype(bf16)
        o_ref[...] = x

    xn = xn_ref[...]
    gate = jnp.dot(xn, wg_ref[...], preferred_element_type=f32)
    up = jnp.dot(xn, wu_ref[...], preferred_element_type=f32)
    h = (gate * jax.nn.sigmoid(gate) * up * 0.5).astype(bf16)
    o_ref[...] += jnp.dot(h, wd_ref[...], preferred_element_type=f32)

    if final_norm:
        @pl.when(j == pl.num_programs(1) - 1)
        def _():
            o_ref[...] = _rms(o_ref[...], fg_ref[...])


def _ffn(x, g, wg, wu, wd, fg, *, layer, final_norm, tm, tf):
    T, D = x.shape
    Fp = wg.shape[2]
    return pl.pallas_call(
        functools.partial(_ffn_body, final_norm=final_norm),
        grid=(T // tm, Fp // tf),
        in_specs=[
            pl.BlockSpec((tm, D), lambda i, j: (i, 0)),
            pl.BlockSpec((1, D), lambda i, j: (0, 0)),
            pl.BlockSpec((None, D, tf), lambda i, j: (layer, 0, j)),
            pl.BlockSpec((None, D, tf), lambda i, j: (layer, 0, j)),
            pl.BlockSpec((None, tf, D), lambda i, j: (layer, j, 0)),
            pl.BlockSpec((1, D), lambda i, j: (0, 0)),
        ],
        out_specs=pl.BlockSpec((tm, D), lambda i, j: (i, 0)),
        out_shape=jax.ShapeDtypeStruct((T, D), f32),
        scratch_shapes=[pltpu.VMEM((tm, D), bf16)],
        compiler_params=pltpu.CompilerParams(
            dimension_semantics=("parallel", "arbitrary"), vmem_limit_bytes=VMEM_LIMIT),
        name="ffn",
    )(x, g, wg, wu, wd, fg)


def _rope(x, cos, sin_signed):
    return x * cos + pltpu.roll(x, HEAD_DIM // 2, 1) * sin_signed


def _inproj_body(x_ref, g_ref, w_ref, wgate_ref, cos_ref, sin_ref, z_ref, gates_ref, xn_ref, *, q_tiles, k_tiles,
                 q_scale):
    j = pl.program_id(1)

    @pl.when(j == 0)
    def _():
        xn = _rms(x_ref[...], g_ref[...]).astype(bf16)
        xn_ref[...] = xn
        gates_ref[...] = jnp.dot(xn, wgate_ref[...], preferred_element_type=f32)

    acc = jnp.dot(xn_ref[...], w_ref[...], preferred_element_type=f32)
    is_q = (j >= q_tiles[0]) & (j < q_tiles[1])
    is_k = (j >= k_tiles[0]) & (j < k_tiles[1])

    @pl.when(is_q | is_k)
    def _():
        sc = jnp.where(is_q, q_scale, 1.0)
        cos = cos_ref[...] * sc
        sin = sin_ref[...] * sc
        for c in range(acc.shape[1] // HEAD_DIM):
            cols = slice(c * HEAD_DIM, (c + 1) * HEAD_DIM)
            z_ref[:, cols] = _rope(acc[:, cols], cos, sin).astype(z_ref.dtype)

    @pl.when(jnp.logical_not(is_q | is_k))
    def _():
        z_ref[...] = acc.astype(z_ref.dtype)


def _inproj(x, g, w, wgate, cos, sin_signed, *, layer, tm, tn, S, q_cols, k_cols, q_scale):
    T, D = x.shape
    N = w.shape[2]
    n_pos = S // tm
    tiles = lambda cols: (cols[0] // tn, cols[1] // tn)
    assert all(c % tn == 0 for c in q_cols + k_cols) and S % tm == 0
    return pl.pallas_call(
        functools.partial(_inproj_body, q_tiles=tiles(q_cols), k_tiles=tiles(k_cols), q_scale=q_scale),
        grid=(T // tm, N // tn),
        in_specs=[
            pl.BlockSpec((tm, D), lambda i, j: (i, 0)),
            pl.BlockSpec((1, D), lambda i, j: (0, 0)),
            pl.BlockSpec((None, D, tn), lambda i, j: (layer, 0, j)),
            pl.BlockSpec((None, D, LANES), lambda i, j: (layer, 0, 0)),
            pl.BlockSpec((tm, HEAD_DIM), lambda i, j: (i % n_pos, 0)),
            pl.BlockSpec((tm, HEAD_DIM), lambda i, j: (i % n_pos, 0)),
        ],
        out_specs=[
            pl.BlockSpec((tm, tn), lambda i, j: (i, j)),
            pl.BlockSpec((tm, LANES), lambda i, j: (i, 0)),
        ],
        out_shape=[jax.ShapeDtypeStruct((T, N), bf16), jax.ShapeDtypeStruct((T, LANES), f32)],
        scratch_shapes=[pltpu.VMEM((tm, D), bf16)],
        compiler_params=pltpu.CompilerParams(
            dimension_semantics=("parallel", "arbitrary"), vmem_limit_bytes=VMEM_LIMIT),
        name="inproj",
    )(x, g, w, wgate, cos, sin_signed)


def _log_sigmoid(x):
    return jnp.minimum(x, 0.0) - jnp.log(1.0 + jnp.exp(-jnp.abs(x)))


def _mlstm_body(qp_ref, kp_ref, v_ref, og_ref, gt_ref, bias_ref, cwq_ref, cwk_ref, gain_ref, out_ref,
                q_s, kt_s, rows_s, cm_s, bb_s, c_s, h_s, *, n_hp):
    S = qp_ref.shape[0]
    L = CHUNK
    nc = S // L
    d = HEAD_DIM

    def conv_silu(x, w):
        row = lax.broadcasted_iota(jnp.int32, x.shape, 0)
        prev = jnp.where(row == 0, 0.0, pltpu.roll(x, 1, 0))
        nxt = jnp.where(row == S - 1, 0.0, pltpu.roll(x, S - 1, 0))
        y = prev * w[0:1, :] + x * w[1:2, :] + nxt * w[2:3, :]
        return y * jax.nn.sigmoid(y)

    r_i = lax.broadcasted_iota(jnp.int32, (L, L), 0)
    c_i = lax.broadcasted_iota(jnp.int32, (L, L), 1)
    lower = c_i <= r_i
    upper = c_i >= r_i
    eye = c_i == r_i

    for hh in range(n_hp):
        cols = slice(hh * d, (hh + 1) * d)
        q_s[:, cols] = conv_silu(qp_ref[:, cols].astype(f32), cwq_ref[:, cols]).astype(bf16)
        k_all = conv_silu(kp_ref[:, cols].astype(f32), cwk_ref[:, cols]) * (d ** -0.5)
        for c in range(nc):
            kt_s[hh, c] = k_all[c * L:(c + 1) * L, :].T.astype(bf16)

        bias = bias_ref[hh]
        ig_f = gt_ref[hh, 0] + bias[0:1, :]
        lf_f = _log_sigmoid(gt_ref[hh, 1] + bias[1:2, :])
        ig_b = gt_ref[hh, 2] + bias[2:3, :]
        lf_b = _log_sigmoid(gt_ref[hh, 3] + bias[3:4, :])
        b_f = jnp.dot(lf_f, upper.astype(f32), preferred_element_type=f32, precision=lax.Precision.HIGHEST)
        b_b = jnp.dot(lf_b, lower.astype(f32), preferred_element_type=f32, precision=lax.Precision.HIGHEST)
        rows_s[hh, 0] = b_f
        rows_s[hh, 1] = ig_f - b_f
        rows_s[hh, 2] = b_b
        rows_s[hh, 3] = ig_b - b_b

    sides = ((0, lower, L - 1), (1, upper, 0))

    def prep(c, carry):
        r0 = pl.multiple_of(c * L, L)
        for hh in range(n_hp):
            for dr, visible, _ in sides:
                b_row = rows_s[hh, 2 * dr, pl.ds(c, 1), :]
                u_row = rows_s[hh, 2 * dr + 1, pl.ds(c, 1), :]
                dm = jnp.where(visible, jnp.broadcast_to(u_row, (L, L)), -jnp.inf)
                cm = jnp.max(dm, axis=1, keepdims=True)
                cm_s[hh, dr, pl.ds(r0, L), :] = jnp.broadcast_to(cm, (L, LANES))
                b_col = jnp.sum(jnp.where(eye, jnp.broadcast_to(b_row, (L, L)), 0.0), axis=1, keepdims=True)
                bb_s[hh, dr, pl.ds(r0, L), :] = jnp.broadcast_to(b_col, (L, LANES))
        return carry

    lax.fori_loop(0, nc, prep, 0)

    c_s[...] = jnp.zeros_like(c_s)
    ones = jnp.ones((L, LANES), bf16)
    twice = lambda a: jnp.concatenate([a, a], axis=1)

    def step(t, ms):
        chains = [(hh, dr, visible, full_row) for hh in range(n_hp) for dr, visible, full_row in sides]
        first, second, m_out = [], [], []
        for (hh, dr, visible, full_row), m in zip(chains, ms):
            cols = slice(hh * d, (hh + 1) * d)
            c = (nc - 1 - t) if dr else t
            r0 = pl.multiple_of(c * L, L)
            u_row = rows_s[hh, 2 * dr + 1, pl.ds(c, 1), :]
            big_m = jnp.maximum(m, cm_s[hh, dr, pl.ds(r0, L), :])
            dm = jnp.where(visible, jnp.broadcast_to(u_row, (L, L)), -jnp.inf)
            w = jnp.exp(dm - big_m)
            floor = jnp.exp(-(bb_s[hh, dr, pl.ds(r0, L), :] + big_m))
            w_inter = jnp.exp(m - big_m)
            m_full = big_m[full_row:full_row + 1, :]
            ws = jnp.exp(u_row - m_full)
            wc = jnp.exp(m - m_full)
            m_out.append(bb_s[hh, dr, pl.ds(r0 + full_row, 1), :] + m_full)

            q = q_s[pl.ds(r0, L), cols]
            kt = kt_s[hh, c]
            v_aug = jnp.concatenate([v_ref[pl.ds(r0, L), cols], ones], axis=1)
            state = c_s[hh, dr]
            s = jnp.dot(q, kt, preferred_element_type=f32)
            inter = jnp.dot(q, state.astype(bf16), preferred_element_type=f32)
            upd = jnp.dot((kt.astype(f32) * ws).astype(bf16), v_aug, preferred_element_type=f32)
            first.append((s, w, v_aug, twice(w_inter) * inter, floor, twice(wc) * state + upd))
        for s, w, v_aug, inter, floor, state in first:
            tot = jnp.dot((s * w).astype(bf16), v_aug, preferred_element_type=f32) + inter
            second.append((tot[:, :d] / jnp.maximum(jnp.abs(tot[:, d:]), floor), state))
        for (hh, dr, _, _), (h, state) in zip(chains, second):
            c = (nc - 1 - t) if dr else t
            h_s[dr, pl.ds(pl.multiple_of(c * L, L), L), hh * d:(hh + 1) * d] = h
            c_s[hh, dr] = state
        return tuple(m_out)

    zero = jnp.zeros((1, LANES), f32)
    lax.fori_loop(0, nc, step, (zero,) * (2 * n_hp))

    for hh in range(n_hp):
        cols = slice(hh * d, (hh + 1) * d)
        hm = _rms(h_s[0, :, cols] + h_s[1, :, cols], gain_ref[:, cols])
        out_ref[:, cols] = (jax.nn.sigmoid(og_ref[:, cols].astype(f32)) * hm).astype(out_ref.dtype)


def _mlstm(z, gt, bias, conv_w, gain, *, n_seq, S, n_heads, n_hp):
    T = z.shape[0]
    d = HEAD_DIM
    nc = S // CHUNK
    G = n_heads // n_hp
    w = n_hp * d
    seq_col = lambda off: pl.BlockSpec((S, w), lambda b, g: (b, off + g))
    return pl.pallas_call(
        functools.partial(_mlstm_body, n_hp=n_hp),
        grid=(n_seq, G),
        in_specs=[
            seq_col(0), seq_col(G), seq_col(2 * G), seq_col(3 * G),
            pl.BlockSpec((None, n_hp, N_GATE_KINDS, nc, CHUNK), lambda b, g: (b, g, 0, 0, 0)),
            pl.BlockSpec((n_hp, N_GATE_KINDS, LANES), lambda b, g: (g, 0, 0)),
            pl.BlockSpec((CONV_W, w), lambda b, g: (0, g)),
            pl.BlockSpec((CONV_W, w), lambda b, g: (0, G + g)),
            pl.BlockSpec((1, w), lambda b, g: (0, g)),
        ],
        out_specs=pl.BlockSpec((S, w), lambda b, g: (b, g)),
        out_shape=jax.ShapeDtypeStruct((T, n_heads * d), bf16),
        scratch_shapes=[
            pltpu.VMEM((S, w), bf16),
            pltpu.VMEM((n_hp, nc, d, CHUNK), bf16),
            pltpu.VMEM((n_hp, N_GATE_KINDS, nc, CHUNK), f32),
            pltpu.VMEM((n_hp, 2, S, LANES), f32),
            pltpu.VMEM((n_hp, 2, S, LANES), f32),
            pltpu.VMEM((n_hp, 2, d, 2 * LANES), f32),
            pltpu.VMEM((2, S, w), f32),
        ],
        compiler_params=pltpu.CompilerParams(
            dimension_semantics=("parallel", "parallel"), vmem_limit_bytes=VMEM_LIMIT),
        name="mlstm",
    )(z, z, z, z, gt, bias, conv_w, conv_w, gain)


def _attn_body(q_ref, k_ref, v_ref, lq1_ref, lk1_ref, lq2_ref, lk2_ref, g_ref, out_ref, *, lambda_init, tk):
    d = HEAD_DIM
    S = k_ref.shape[0]
    nt = (((1,), (1,)), ((), ()))

    def component(q, col):
        m = l = acc = None
        for c in range(S // tk):
            rows = slice(c * tk, (c + 1) * tk)
            s = lax.dot_general(q, k_ref[rows, col:col + d], nt, preferred_element_type=f32)
            m_c = jnp.max(s, axis=1, keepdims=True)
            if c == 0:
                m = m_c
                p = jnp.exp2(s - m)
                l = jnp.sum(p, axis=1, keepdims=True)
                acc = jnp.dot(p.astype(bf16), v_ref[rows, :], preferred_element_type=f32)
            else:
                m_new = jnp.maximum(m, m_c)
                alpha = jnp.exp2(m - m_new)
                p = jnp.exp2(s - m_new)
                l = alpha * l + jnp.sum(p, axis=1, keepdims=True)
                acc = alpha * acc + jnp.dot(p.astype(bf16), v_ref[rows, :], preferred_element_type=f32)
                m = m_new
        return acc, l

    acc1, l1 = component(q_ref[:, :d], 0)
    acc2, l2 = component(q_ref[:, d:], d)
    lam = (jnp.exp(jnp.sum(lq1_ref[...] * lk1_ref[...], axis=1, keepdims=True))
           - jnp.exp(jnp.sum(lq2_ref[...] * lk2_ref[...], axis=1, keepdims=True)) + lambda_init)
    o = acc1 * (1.0 / l1) - acc2 * (lam / l2)
    o = _rms(o, g_ref[...]) * (1.0 - lambda_init)
    out_ref[...] = o.astype(out_ref.dtype)


def _attn(z, lq1, lk1, lq2, lk2, g, *, n_seq, S, n_heads, col0, lambda_init, tq, tk):
    T = z.shape[0]
    d = HEAD_DIM
    nq = S // tq
    A = n_heads
    row = lambda b, h, i: (0, 0)
    return pl.pallas_call(
        functools.partial(_attn_body, lambda_init=lambda_init, tk=tk),
        grid=(n_seq, A, nq),
        in_specs=[
            pl.BlockSpec((tq, 2 * d), lambda b, h, i: (b * nq + i, col0 + h)),
            pl.BlockSpec((S, 2 * d), lambda b, h, i: (b, col0 + A + h)),
            pl.BlockSpec((S, 2 * d), lambda b, h, i: (b, col0 + 2 * A + h)),
            pl.BlockSpec((1, d), row), pl.BlockSpec((1, d), row),
            pl.BlockSpec((1, d), row), pl.BlockSpec((1, d), row),
            pl.BlockSpec((1, 2 * d), row),
        ],
        out_specs=pl.BlockSpec((tq, 2 * d), lambda b, h, i: (b * nq + i, h)),
        out_shape=jax.ShapeDtypeStruct((T, A * 2 * d), bf16),
        compiler_params=pltpu.CompilerParams(
            dimension_semantics=("parallel", "parallel", "parallel"), vmem_limit_bytes=VMEM_LIMIT),
        name="diffattn",
    )(z, z, z, lq1, lk1, lq2, lk2, g)


def _outproj_body(x_ref, om_ref, oa_ref, wm_ref, wa_ref, o_ref):
    o_ref[...] = (x_ref[...] + jnp.dot(om_ref[...], wm_ref[...], preferred_element_type=f32)
                  + jnp.dot(oa_ref[...], wa_ref[...], preferred_element_type=f32))


def _outproj(x, om, oa, w, *, layer, tm):
    T, D = x.shape
    Wm, Wa = om.shape[1], oa.shape[1]
    assert Wm == Wa and Wm + Wa == D
    return pl.pallas_call(
        _outproj_body,
        grid=(T // tm,),
        in_specs=[
            pl.BlockSpec((tm, D), lambda i: (i, 0)),
            pl.BlockSpec((tm, Wm), lambda i: (i, 0)),
            pl.BlockSpec((tm, Wa), lambda i: (i, 0)),
            pl.BlockSpec((None, Wm, D), lambda i: (layer, 0, 0)),
            pl.BlockSpec((None, Wa, D), lambda i: (layer, 1, 0)),
        ],
        out_specs=pl.BlockSpec((tm, D), lambda i: (i, 0)),
        out_shape=jax.ShapeDtypeStruct((T, D), f32),
        compiler_params=pltpu.CompilerParams(
            dimension_semantics=("parallel",), vmem_limit_bytes=VMEM_LIMIT),
        name="outproj",
    )(x, om, oa, w, w)


def _tiles(T, S):
    g = math.gcd(T, S)
    return math.gcd(512, g), math.gcd(1024, g), math.gcd(1024, g), math.gcd(512, S)


def kernel(x_prompt, x_sample, ffn1_norm, ffn1_w_gate, ffn1_w_up, ffn1_w_down, mix_norm, w_in, conv_qk, b_gate,
           mlstm_norm, lambda_q1, lambda_k1, lambda_q2, lambda_k2, diff_subln, w_out, ffn2_norm, ffn2_w_gate,
           ffn2_w_up, ffn2_w_down, final_norm):
    depth, D, F = ffn1_w_gate.shape
    S = x_prompt.shape[1]
    assert x_sample.shape[1] == S and S % CHUNK == 0
    d = HEAD_DIM
    m_w = D // 2
    MH = m_w // d
    a_w = D - m_w
    AH = a_w // (2 * d)
    n_gates = N_GATE_KINDS * MH
    assert n_gates <= LANES and w_in.shape[2] == 4 * m_w + n_gates + 3 * a_w
    off_g = 4 * m_w
    n_seq = x_prompt.shape[0] + x_sample.shape[0]
    T = n_seq * S
    tm, tm_ffn, tm_in, tq = _tiles(T, S)
    tf = min(256, _round_up(F, LANES))
    Fp = _round_up(F, tf)
    n_main = 4 * m_w + 3 * a_w
    tn = 1024 if n_main % 1024 == 0 else 256
    nc = S // CHUNK

    x = jnp.concatenate([x_prompt.reshape(-1, D), x_sample.reshape(-1, D)], axis=0)

    inv = 1.0 / (ROPE_THETA ** (jnp.arange(0, d, 2, dtype=f32) / d))
    ang = jnp.arange(S, dtype=f32)[:, None] * inv[None, :]
    emb = jnp.concatenate([ang, ang], axis=-1)
    cos = jnp.cos(emb)
    sin_signed = jnp.sin(emb) * jnp.concatenate([-jnp.ones((d // 2,), f32), jnp.ones((d // 2,), f32)])

    def pad_ff(w, axis):
        pad = [(0, 0), (0, 0), (0, 0)]
        pad[axis] = (0, Fp - F)
        return jnp.pad(w.astype(bf16), pad)

    ffn_w = [(pad_ff(g, 2), pad_ff(u, 2), pad_ff(dn, 1)) for g, u, dn in
             ((ffn1_w_gate, ffn1_w_up, ffn1_w_down), (ffn2_w_gate, ffn2_w_up, ffn2_w_down))]
    ffn_g = (ffn1_norm, ffn2_norm)
    w_main = jnp.concatenate([w_in[:, :, :off_g], w_in[:, :, off_g + n_gates:]], axis=2).astype(bf16)
    w_gate = w_in[:, :, off_g:off_g + n_gates].reshape(depth, D, N_GATE_KINDS, MH).transpose(0, 1, 3, 2)
    w_gate = jnp.pad(w_gate.reshape(depth, D, n_gates), ((0, 0), (0, 0), (0, LANES - n_gates))).astype(bf16)
    w_o = w_out.astype(bf16)

    row = lambda v: v.reshape(1, -1).astype(f32)

    def ffn(x, which, l):
        wg, wu, wd = ffn_w[which]
        return _ffn(x, row(ffn_g[which][l]), wg, wu, wd, row(final_norm), layer=l,
                    final_norm=(which == 1 and l == depth - 1), tm=tm_ffn, tf=tf)

    for l in range(depth):
        lambda_init = 0.8 - 0.6 * math.exp(-0.3 * l)
        x = ffn(x, 0, l)
        z, gates = _inproj(x, row(mix_norm[l]), w_main, w_gate, cos, sin_signed, layer=l, tm=tm_in, tn=tn, S=S,
                           q_cols=(4 * m_w, 4 * m_w + a_w), k_cols=(4 * m_w + a_w, 4 * m_w + 2 * a_w),
                           q_scale=d ** -0.5 * math.log2(math.e))

        gt = gates[:, :n_gates].reshape(n_seq, nc, CHUNK, MH, N_GATE_KINDS).transpose(0, 3, 4, 1, 2)
        bias = jnp.broadcast_to(b_gate[l].astype(f32).T[:, :, None], (MH, N_GATE_KINDS, LANES))
        om = _mlstm(z, gt, bias, conv_qk[l].astype(f32), row(mlstm_norm[l]), n_seq=n_seq, S=S, n_heads=MH,
                    n_hp=math.gcd(2, MH))
        oa = _attn(z, row(lambda_q1[l]), row(lambda_k1[l]), row(lambda_q2[l]), row(lambda_k2[l]),
                   row(diff_subln[l]), n_seq=n_seq, S=S, n_heads=AH, col0=4 * m_w // (2 * d),
                   lambda_init=lambda_init, tq=tq, tk=tq)
        x = _outproj(x, om, oa, w_o, layer=l, tm=tm)
        x = ffn(x, 1, l)

    n_p = x_prompt.shape[0] * S
    return x[:n_p].reshape(x_prompt.shape), x[n_p:].reshape(x_sample.shape)
```

```python
import functools
import math

import jax
import jax.numpy as jnp
from jax import lax
from jax.experimental import pallas as pl
from jax.experimental.pallas import tpu as pltpu

EPS = 1e-6
HEAD_DIM = 128
CHUNK = 128
CONV_W = 3
ROPE_THETA = 10000.0
N_GATE_KINDS = 4
LANES = 128
VMEM_LIMIT = 56 * 1024 * 1024

f32 = jnp.float32
bf16 = jnp.bfloat16


def _round_up(n, m):
    return (n + m - 1) // m * m


def _rms(x, g):
    ms = jnp.mean(x * x, axis=-1, keepdims=True)
    return x * lax.rsqrt(ms + EPS) * g


def _part_tiles(rows, tm):
    tiles, lo = [], 0
    for r in rows:
        assert r % tm == 0
        tiles.append((lo, lo + r // tm))
        lo += r // tm
    return tuple(tiles)


def _ffn_body(*refs, in_tiles, out_tiles, final_norm, f_last):
    n_in, n_out = len(in_tiles), len(out_tiles)
    x_refs = refs[:n_in]
    g_ref, wg_ref, wu_ref, wd_ref, fg_ref = refs[n_in:n_in + 5]
    o_refs = refs[n_in + 5:n_in + 5 + n_out]
    xn_ref = refs[-1]
    i = pl.program_id(0)
    j = pl.program_id(1)
    last = pl.num_programs(1) - 1
    tf = wg_ref.shape[1]

    def in_part(tiles, p, fn):
        if len(tiles) == 1:
            fn()
        else:
            pl.when((i >= tiles[p][0]) & (i < tiles[p][1]))(fn)

    def for_each_out(fn):
        for p in range(n_out):
            in_part(out_tiles, p, functools.partial(fn, o_refs[p]))

    @pl.when(j == 0)
    def _():
        for p in range(n_in):
            def start(x_ref=x_refs[p]):
                x = x_ref[...]
                xn_ref[...] = _rms(x, g_ref[...]).astype(bf16)

                def init(o_ref):
                    o_ref[...] = x
                for_each_out(init)
            in_part(in_tiles, p, start)

    def accumulate(nv):
        xn = xn_ref[...]
        gate = jnp.dot(xn, wg_ref[:, :nv], preferred_element_type=f32)
        up = jnp.dot(xn, wu_ref[:, :nv], preferred_element_type=f32)
        h = (gate * jax.nn.sigmoid(gate) * up * 0.5).astype(bf16)
        delta = jnp.dot(h, wd_ref[:nv, :], preferred_element_type=f32)

        def add(o_ref):
            o_ref[...] += delta
        for_each_out(add)

    if f_last == tf:
        accumulate(tf)
    else:
        pl.when(j < last)(functools.partial(accumulate, tf))
        pl.when(j == last)(functools.partial(accumulate, f_last))

    if final_norm:
        @pl.when(j == last)
        def _():
            def norm(o_ref):
                o_ref[...] = _rms(o_ref[...], fg_ref[...])
            for_each_out(norm)


def _ffn(xs, g, wg, wu, wd, fg, *, layer, final_norm, tm, tf, out_rows):
    D = xs[0].shape[1]
    T = sum(x.shape[0] for x in xs)
    F = wg.shape[2]
    n_f = pl.cdiv(F, tf)
    in_tiles = _part_tiles([x.shape[0] for x in xs], tm)
    out_tiles = _part_tiles(out_rows, tm)
    assert sum(out_rows) == T

    def part_spec(lo, hi):
        return pl.BlockSpec((tm, D), lambda i, j: (jnp.clip(i - lo, 0, hi - lo - 1), 0))

    outs = pl.pallas_call(
        functools.partial(_ffn_body, in_tiles=in_tiles, out_tiles=out_tiles, final_norm=final_norm,
                          f_last=F - (n_f - 1) * tf),
        grid=(T // tm, n_f),
        in_specs=[part_spec(lo, hi) for lo, hi in in_tiles] + [
            pl.BlockSpec((1, D), lambda i, j: (0, 0)),
            pl.BlockSpec((None, D, tf), lambda i, j: (layer, 0, j)),
            pl.BlockSpec((None, D, tf), lambda i, j: (layer, 0, j)),
            pl.BlockSpec((None, tf, D), lambda i, j: (layer, j, 0)),
            pl.BlockSpec((1, D), lambda i, j: (0, 0)),
        ],
        out_specs=[part_spec(lo, hi) for lo, hi in out_tiles],
        out_shape=[jax.ShapeDtypeStruct((r, D), f32) for r in out_rows],
        scratch_shapes=[pltpu.VMEM((tm, D), bf16)],
        compiler_params=pltpu.CompilerParams(
            dimension_semantics=("arbitrary", "arbitrary"), vmem_limit_bytes=VMEM_LIMIT),
        name="ffn",
    )(*xs, g, wg, wu, wd, fg)
    return outs


def _rope(x, cos, sin_signed):
    return x * cos + pltpu.roll(x, HEAD_DIM // 2, 1) * sin_signed


def _inproj_body(x_ref, g_ref, w_ref, wgate_ref, cos_ref, sin_ref, z_ref, gates_ref, xn_ref, *, q_tiles, k_tiles,
                 q_scale):
    j = pl.program_id(1)

    @pl.when(j == 0)
    def _():
        xn = _rms(x_ref[...], g_ref[...]).astype(bf16)
        xn_ref[...] = xn
        gates_ref[...] = jnp.dot(xn, wgate_ref[...], preferred_element_type=f32)

    acc = jnp.dot(xn_ref[...], w_ref[...], preferred_element_type=f32)
    is_q = (j >= q_tiles[0]) & (j < q_tiles[1])
    is_k = (j >= k_tiles[0]) & (j < k_tiles[1])

    @pl.when(is_q | is_k)
    def _():
        sc = jnp.where(is_q, q_scale, 1.0)
        cos = cos_ref[...] * sc
        sin = sin_ref[...] * sc
        for c in range(acc.shape[1] // HEAD_DIM):
            cols = slice(c * HEAD_DIM, (c + 1) * HEAD_DIM)
            z_ref[:, cols] = _rope(acc[:, cols], cos, sin).astype(z_ref.dtype)

    @pl.when(jnp.logical_not(is_q | is_k))
    def _():
        z_ref[...] = acc.astype(z_ref.dtype)


def _inproj(x, g, w, wgate, cos, sin_signed, *, layer, tm, tn, S, q_cols, k_cols, q_scale):
    T, D = x.shape
    N = w.shape[2]
    n_pos = S // tm
    tiles = lambda cols: (cols[0] // tn, cols[1] // tn)
    assert all(c % tn == 0 for c in q_cols + k_cols) and S % tm == 0
    return pl.pallas_call(
        functools.partial(_inproj_body, q_tiles=tiles(q_cols), k_tiles=tiles(k_cols), q_scale=q_scale),
        grid=(T // tm, N // tn),
        in_specs=[
            pl.BlockSpec((tm, D), lambda i, j: (i, 0)),
            pl.BlockSpec((1, D), lambda i, j: (0, 0)),
            pl.BlockSpec((None, D, tn), lambda i, j: (layer, 0, j)),
            pl.BlockSpec((None, D, LANES), lambda i, j: (layer, 0, 0)),
            pl.BlockSpec((tm, HEAD_DIM), lambda i, j: (i % n_pos, 0)),
            pl.BlockSpec((tm, HEAD_DIM), lambda i, j: (i % n_pos, 0)),
        ],
        out_specs=[
            pl.BlockSpec((tm, tn), lambda i, j: (i, j)),
            pl.BlockSpec((tm, LANES), lambda i, j: (i, 0)),
        ],
        out_shape=[jax.ShapeDtypeStruct((T, N), bf16), jax.ShapeDtypeStruct((T, LANES), f32)],
        scratch_shapes=[pltpu.VMEM((tm, D), bf16)],
        compiler_params=pltpu.CompilerParams(
            dimension_semantics=("parallel", "arbitrary"), vmem_limit_bytes=VMEM_LIMIT),
        name="inproj",
    )(x, g, w, wgate, cos, sin_signed)


def _log_sigmoid(x):
    return jnp.minimum(x, 0.0) - jnp.log(1.0 + jnp.exp(-jnp.abs(x)))


def _mlstm_body(qp_ref, kp_ref, v_ref, og_ref, gt_ref, bias_ref, cwq_ref, cwk_ref, gain_ref, out_ref,
                q_s, kt_s, rows_s, cm_s, bb_s, c_s, h_s, *, n_hp):
    S = qp_ref.shape[0]
    L = CHUNK
    nc = S // L
    d = HEAD_DIM

    def conv_silu(x, w):
        row = lax.broadcasted_iota(jnp.int32, x.shape, 0)
        prev = jnp.where(row == 0, 0.0, pltpu.roll(x, 1, 0))
        nxt = jnp.where(row == S - 1, 0.0, pltpu.roll(x, S - 1, 0))
        y = prev * w[0:1, :] + x * w[1:2, :] + nxt * w[2:3, :]
        return y * jax.nn.sigmoid(y)

    r_i = lax.broadcasted_iota(jnp.int32, (L, L), 0)
    c_i = lax.broadcasted_iota(jnp.int32, (L, L), 1)
    lower = c_i <= r_i
    upper = c_i >= r_i
    eye = c_i == r_i

    for hh in range(n_hp):
        cols = slice(hh * d, (hh + 1) * d)
        q_s[:, cols] = conv_silu(qp_ref[:, cols].astype(f32), cwq_ref[:, cols]).astype(bf16)
        k_all = conv_silu(kp_ref[:, cols].astype(f32), cwk_ref[:, cols]) * (d ** -0.5)
        for c in range(nc):
            kt_s[hh, c] = k_all[c * L:(c + 1) * L, :].T.astype(bf16)

        bias = bias_ref[hh]
        ig_f = gt_ref[hh, 0] + bias[0:1, :]
        lf_f = _log_sigmoid(gt_ref[hh, 1] + bias[1:2, :])
        ig_b = gt_ref[hh, 2] + bias[2:3, :]
        lf_b = _log_sigmoid(gt_ref[hh, 3] + bias[3:4, :])
        b_f = jnp.dot(lf_f, upper.astype(f32), preferred_element_type=f32, precision=lax.Precision.HIGHEST)
        b_b = jnp.dot(lf_b, lower.astype(f32), preferred_element_type=f32, precision=lax.Precision.HIGHEST)
        rows_s[hh, 0] = b_f
        rows_s[hh, 1] = ig_f - b_f
        rows_s[hh, 2] = b_b
        rows_s[hh, 3] = ig_b - b_b

    sides = ((0, lower, L - 1), (1, upper, 0))

    def prep(c, carry):
        r0 = pl.multiple_of(c * L, L)
        for hh in range(n_hp):
            for dr, visible, _ in sides:
                b_row = rows_s[hh, 2 * dr, pl.ds(c, 1), :]
                u_row = rows_s[hh, 2 * dr + 1, pl.ds(c, 1), :]
                dm = jnp.where(visible, jnp.broadcast_to(u_row, (L, L)), -jnp.inf)
                cm = jnp.max(dm, axis=1, keepdims=True)
                cm_s[hh, dr, pl.ds(r0, L), :] = jnp.broadcast_to(cm, (L, LANES))
                b_col = jnp.sum(jnp.where(eye, jnp.broadcast_to(b_row, (L, L)), 0.0), axis=1, keepdims=True)
                bb_s[hh, dr, pl.ds(r0, L), :] = jnp.broadcast_to(b_col, (L, LANES))
        return carry

    lax.fori_loop(0, nc, prep, 0)

    c_s[...] = jnp.zeros_like(c_s)
    ones = jnp.ones((L, LANES), bf16)
    twice = lambda a: jnp.concatenate([a, a], axis=1)

    def step(t, ms):
        chains = [(hh, dr, visible, full_row) for hh in range(n_hp) for dr, visible, full_row in sides]
        first, second, m_out = [], [], []
        for (hh, dr, visible, full_row), m in zip(chains, ms):
            cols = slice(hh * d, (hh + 1) * d)
            c = (nc - 1 - t) if dr else t
            r0 = pl.multiple_of(c * L, L)
            u_row = rows_s[hh, 2 * dr + 1, pl.ds(c, 1), :]
            big_m = jnp.maximum(m, cm_s[hh, dr, pl.ds(r0, L), :])
            dm = jnp.where(visible, jnp.broadcast_to(u_row, (L, L)), -jnp.inf)
            w = jnp.exp(dm - big_m)
            floor = jnp.exp(-(bb_s[hh, dr, pl.ds(r0, L), :] + big_m))
            w_inter = jnp.exp(m - big_m)
            m_full = big_m[full_row:full_row + 1, :]
            ws = jnp.exp(u_row - m_full)
            wc = jnp.exp(m - m_full)
            m_out.append(bb_s[hh, dr, pl.ds(r0 + full_row, 1), :] + m_full)

            q = q_s[pl.ds(r0, L), cols]
            kt = kt_s[hh, c]
            v_aug = jnp.concatenate([v_ref[pl.ds(r0, L), cols], ones], axis=1)
            state = c_s[hh, dr]
            s = jnp.dot(q, kt, preferred_element_type=f32)
            inter = jnp.dot(q, state.astype(bf16), preferred_element_type=f32)
            upd = jnp.dot((kt.astype(f32) * ws).astype(bf16), v_aug, preferred_element_type=f32)
            first.append((s, w, v_aug, twice(w_inter) * inter, floor, twice(wc) * state + upd))
        for s, w, v_aug, inter, floor, state in first:
            tot = jnp.dot((s * w).astype(bf16), v_aug, preferred_element_type=f32) + inter
            second.append((tot[:, :d] / jnp.maximum(jnp.abs(tot[:, d:]), floor), state))
        for (hh, dr, _, _), (h, state) in zip(chains, second):
            c = (nc - 1 - t) if dr else t
            h_s[dr, pl.ds(pl.multiple_of(c * L, L), L), hh * d:(hh + 1) * d] = h
            c_s[hh, dr] = state
        return tuple(m_out)

    zero = jnp.zeros((1, LANES), f32)
    lax.fori_loop(0, nc, step, (zero,) * (2 * n_hp))

    for hh in range(n_hp):
        cols = slice(hh * d, (hh + 1) * d)
        hm = _rms(h_s[0, :, cols] + h_s[1, :, cols], gain_ref[:, cols])
        out_ref[:, cols] = (jax.nn.sigmoid(og_ref[:, cols].astype(f32)) * hm).astype(out_ref.dtype)


def _mlstm(z, gt, bias, conv_w, gain, *, n_seq, S, n_heads, n_hp):
    T = z.shape[0]
    d = HEAD_DIM
    nc = S // CHUNK
    G = n_heads // n_hp
    w = n_hp * d
    seq_col = lambda off: pl.BlockSpec((S, w), lambda b, g: (b, off + g))
    return pl.pallas_call(
        functools.partial(_mlstm_body, n_hp=n_hp),
        grid=(n_seq, G),
        in_specs=[
            seq_col(0), seq_col(G), seq_col(2 * G), seq_col(3 * G),
            pl.BlockSpec((None, n_hp, N_GATE_KINDS, nc, CHUNK), lambda b, g: (b, g, 0, 0, 0)),
            pl.BlockSpec((n_hp, N_GATE_KINDS, LANES), lambda b, g: (g, 0, 0)),
            pl.BlockSpec((CONV_W, w), lambda b, g: (0, g)),
            pl.BlockSpec((CONV_W, w), lambda b, g: (0, G + g)),
            pl.BlockSpec((1, w), lambda b, g: (0, g)),
        ],
        out_specs=pl.BlockSpec((S, w), lambda b, g: (b, g)),
        out_shape=jax.ShapeDtypeStruct((T, n_heads * d), bf16),
        scratch_shapes=[
            pltpu.VMEM((S, w), bf16),
            pltpu.VMEM((n_hp, nc, d, CHUNK), bf16),
            pltpu.VMEM((n_hp, N_GATE_KINDS, nc, CHUNK), f32),
            pltpu.VMEM((n_hp, 2, S, LANES), f32),
            pltpu.VMEM((n_hp, 2, S, LANES), f32),
            pltpu.VMEM((n_hp, 2, d, 2 * LANES), f32),
            pltpu.VMEM((2, S, w), f32),
        ],
        compiler_params=pltpu.CompilerParams(
            dimension_semantics=("parallel", "parallel"), vmem_limit_bytes=VMEM_LIMIT),
        name="mlstm",
    )(z, z, z, z, gt, bias, conv_w, conv_w, gain)


def _attn_body(q_ref, k_ref, v_ref, lq1_ref, lk1_ref, lq2_ref, lk2_ref, g_ref, out_ref, *, lambda_init, tk):
    d = HEAD_DIM
    S = k_ref.shape[0]
    nt = (((1,), (1,)), ((), ()))

    def component(q, col):
        m = l = acc = None
        for c in range(S // tk):
            rows = slice(c * tk, (c + 1) * tk)
            s = lax.dot_general(q, k_ref[rows, col:col + d], nt, preferred_element_type=f32)
            m_c = jnp.max(s, axis=1, keepdims=True)
            if c == 0:
                m = m_c
                p = jnp.exp2(s - m)
                l = jnp.sum(p, axis=1, keepdims=True)
                acc = jnp.dot(p.astype(bf16), v_ref[rows, :], preferred_element_type=f32)
            else:
                m_new = jnp.maximum(m, m_c)
                alpha = jnp.exp2(m - m_new)
                p = jnp.exp2(s - m_new)
                l = alpha * l + jnp.sum(p, axis=1, keepdims=True)
                acc = alpha * acc + jnp.dot(p.astype(bf16), v_ref[rows, :], preferred_element_type=f32)
                m = m_new
        return acc, l

    acc1, l1 = component(q_ref[:, :d], 0)
    acc2, l2 = component(q_ref[:, d:], d)
    lam = (jnp.exp(jnp.sum(lq1_ref[...] * lk1_ref[...], axis=1, keepdims=True))
           - jnp.exp(jnp.sum(lq2_ref[...] * lk2_ref[...], axis=1, keepdims=True)) + lambda_init)
    o = acc1 * (1.0 / l1) - acc2 * (lam / l2)
    o = _rms(o, g_ref[...]) * (1.0 - lambda_init)
    out_ref[...] = o.astype(out_ref.dtype)


def _attn(z, lq1, lk1, lq2, lk2, g, *, n_seq, S, n_heads, col0, lambda_init, tq, tk):
    T = z.shape[0]
    d = HEAD_DIM
    nq = S // tq
    A = n_heads
    row = lambda b, h, i: (0, 0)
    return pl.pallas_call(
        functools.partial(_attn_body, lambda_init=lambda_init, tk=tk),
        grid=(n_seq, A, nq),
        in_specs=[
            pl.BlockSpec((tq, 2 * d), lambda b, h, i: (b * nq + i, col0 + h)),
            pl.BlockSpec((S, 2 * d), lambda b, h, i: (b, col0 + A + h)),
            pl.BlockSpec((S, 2 * d), lambda b, h, i: (b, col0 + 2 * A + h)),
            pl.BlockSpec((1, d), row), pl.BlockSpec((1, d), row),
            pl.BlockSpec((1, d), row), pl.BlockSpec((1, d), row),
            pl.BlockSpec((1, 2 * d), row),
        ],
        out_specs=pl.BlockSpec((tq, 2 * d), lambda b, h, i: (b * nq + i, h)),
        out_shape=jax.ShapeDtypeStruct((T, A * 2 * d), bf16),
        compiler_params=pltpu.CompilerParams(
            dimension_semantics=("parallel", "parallel", "parallel"), vmem_limit_bytes=VMEM_LIMIT),
        name="diffattn",
    )(z, z, z, lq1, lk1, lq2, lk2, g)


def _outproj_body(x_ref, om_ref, oa_ref, wm_ref, wa_ref, o_ref):
    o_ref[...] = (x_ref[...] + jnp.dot(om_ref[...], wm_ref[...], preferred_element_type=f32)
                  + jnp.dot(oa_ref[...], wa_ref[...], preferred_element_type=f32))


def _outproj(x, om, oa, w, *, layer, tm):
    T, D = x.shape
    Wm, Wa = om.shape[1], oa.shape[1]
    assert Wm == Wa and Wm + Wa == D
    return pl.pallas_call(
        _outproj_body,
        grid=(T // tm,),
        in_specs=[
            pl.BlockSpec((tm, D), lambda i: (i, 0)),
            pl.BlockSpec((tm, Wm), lambda i: (i, 0)),
            pl.BlockSpec((tm, Wa), lambda i: (i, 0)),
            pl.BlockSpec((None, Wm, D), lambda i: (layer, 0, 0)),
            pl.BlockSpec((None, Wa, D), lambda i: (layer, 1, 0)),
        ],
        out_specs=pl.BlockSpec((tm, D), lambda i: (i, 0)),
        out_shape=jax.ShapeDtypeStruct((T, D), f32),
        compiler_params=pltpu.CompilerParams(
            dimension_semantics=("parallel",), vmem_limit_bytes=VMEM_LIMIT),
        name="outproj",
    )(x, om, oa, w, w)


def _tiles(T, S):
    g = math.gcd(T, S)
    return math.gcd(512, g), math.gcd(512, g), math.gcd(1024, g), math.gcd(512, S), math.gcd(1024, S)


def kernel(x_prompt, x_sample, ffn1_norm, ffn1_w_gate, ffn1_w_up, ffn1_w_down, mix_norm, w_in, conv_qk, b_gate,
           mlstm_norm, lambda_q1, lambda_k1, lambda_q2, lambda_k2, diff_subln, w_out, ffn2_norm, ffn2_w_gate,
           ffn2_w_up, ffn2_w_down, final_norm):
    depth, D, F = ffn1_w_gate.shape
    S = x_prompt.shape[1]
    assert x_sample.shape[1] == S and S % CHUNK == 0
    d = HEAD_DIM
    m_w = D // 2
    MH = m_w // d
    a_w = D - m_w
    AH = a_w // (2 * d)
    n_gates = N_GATE_KINDS * MH
    assert n_gates <= LANES and w_in.shape[2] == 4 * m_w + n_gates + 3 * a_w
    off_g = 4 * m_w
    n_seq = x_prompt.shape[0] + x_sample.shape[0]
    T = n_seq * S
    tm, tm_ffn, tm_in, tq, tk = _tiles(T, S)
    tf = min(512, _round_up(F, LANES))
    n_main = 4 * m_w + 3 * a_w
    tn = 1024 if n_main % 1024 == 0 else 256
    nc = S // CHUNK
    n_p = x_prompt.shape[0] * S

    inv = 1.0 / (ROPE_THETA ** (jnp.arange(0, d, 2, dtype=f32) / d))
    ang = jnp.arange(S, dtype=f32)[:, None] * inv[None, :]
    emb = jnp.concatenate([ang, ang], axis=-1)
    cos = jnp.cos(emb)
    sin_signed = jnp.sin(emb) * jnp.concatenate([-jnp.ones((d // 2,), f32), jnp.ones((d // 2,), f32)])

    ffn_w = [tuple(w.astype(bf16) for w in ws) for ws in
             ((ffn1_w_gate, ffn1_w_up, ffn1_w_down), (ffn2_w_gate, ffn2_w_up, ffn2_w_down))]
    ffn_g = (ffn1_norm, ffn2_norm)
    w_in_b = w_in.astype(bf16)
    w_main = jnp.concatenate([w_in_b[:, :, :off_g], w_in_b[:, :, off_g + n_gates:]], axis=2)
    w_gate = w_in_b[:, :, off_g:off_g + n_gates].reshape(depth, D, N_GATE_KINDS, MH).transpose(0, 1, 3, 2)
    w_gate = jnp.pad(w_gate.reshape(depth, D, n_gates), ((0, 0), (0, 0), (0, LANES - n_gates)))
    w_o = w_out.astype(bf16)

    row = lambda v: v.reshape(1, -1).astype(f32)

    def ffn(xs, which, l, out_rows):
        wg, wu, wd = ffn_w[which]
        return _ffn(xs, row(ffn_g[which][l]), wg, wu, wd, row(final_norm), layer=l,
                    final_norm=(which == 1 and l == depth - 1), tm=tm_ffn, tf=tf, out_rows=out_rows)

    xs = (x_prompt.reshape(-1, D), x_sample.reshape(-1, D))
    for l in range(depth):
        lambda_init = 0.8 - 0.6 * math.exp(-0.3 * l)
        (x,) = ffn(xs, 0, l, (T,))
        z, gates = _inproj(x, row(mix_norm[l]), w_main, w_gate, cos, sin_signed, layer=l, tm=tm_in, tn=tn, S=S,
                           q_cols=(4 * m_w, 4 * m_w + a_w), k_cols=(4 * m_w + a_w, 4 * m_w + 2 * a_w),
                           q_scale=d ** -0.5 * math.log2(math.e))

        gt = gates[:, :n_gates].reshape(n_seq, nc, CHUNK, MH, N_GATE_KINDS).transpose(0, 3, 4, 1, 2)
        bias = jnp.broadcast_to(b_gate[l].astype(f32).T[:, :, None], (MH, N_GATE_KINDS, LANES))
        om = _mlstm(z, gt, bias, conv_qk[l].astype(f32), row(mlstm_norm[l]), n_seq=n_seq, S=S, n_heads=MH,
                    n_hp=math.gcd(2, MH))
        oa = _attn(z, row(lambda_q1[l]), row(lambda_k1[l]), row(lambda_q2[l]), row(lambda_k2[l]),
                   row(diff_subln[l]), n_seq=n_seq, S=S, n_heads=AH, col0=4 * m_w // (2 * d),
                   lambda_init=lambda_init, tq=tq, tk=tk)
        x = _outproj(x, om, oa, w_o, layer=l, tm=tm)
        xs = ffn((x,), 1, l, (n_p, T - n_p) if l == depth - 1 else (T,))

    return xs[0].reshape(x_prompt.shape), xs[1].reshape(x_sample.shape)
```

```python
import functools
import math

import jax
import jax.numpy as jnp
from jax import lax
from jax.experimental import pallas as pl
from jax.experimental.pallas import tpu as pltpu

EPS = 1e-6
HEAD_DIM = 128
CHUNK = 128
CONV_W = 3
ROPE_THETA = 10000.0
N_GATE_KINDS = 4
LANES = 128
VMEM_LIMIT = 56 * 1024 * 1024

f32 = jnp.float32
bf16 = jnp.bfloat16


def _round_up(n, m):
    return (n + m - 1) // m * m


def _rms(x, g):
    ms = jnp.mean(x * x, axis=-1, keepdims=True)
    return x * lax.rsqrt(ms + EPS) * g


def _part_tiles(rows, tm):
    tiles, lo = [], 0
    for r in rows:
        assert r % tm == 0
        tiles.append((lo, lo + r // tm))
        lo += r // tm
    return tuple(tiles)


def _ffn_body(*refs, in_tiles, out_tiles, final_norm, f_last):
    n_in, n_out = len(in_tiles), len(out_tiles)
    x_refs = refs[:n_in]
    g_ref, wg_ref, wu_ref, wd_ref, fg_ref = refs[n_in:n_in + 5]
    o_refs = refs[n_in + 5:n_in + 5 + n_out]
    xn_ref = refs[n_in + 5 + n_out]
    acc_ref = o_refs[0] if n_out == 1 else refs[n_in + 6 + n_out]
    i = pl.program_id(0)
    j = pl.program_id(1)
    last = pl.num_programs(1) - 1
    tf = wg_ref.shape[1]

    def in_part(tiles, p, fn):
        if len(tiles) == 1:
            fn()
        else:
            pl.when((i >= tiles[p][0]) & (i < tiles[p][1]))(fn)

    @pl.when(j == 0)
    def _():
        for p in range(n_in):
            def start(x_ref=x_refs[p]):
                x = x_ref[...]
                xn_ref[...] = _rms(x, g_ref[...]).astype(bf16)
                acc_ref[...] = x
            in_part(in_tiles, p, start)

    def accumulate(nv):
        xn = xn_ref[...]
        gate = jnp.dot(xn, wg_ref[:, :nv], preferred_element_type=f32)
        up = jnp.dot(xn, wu_ref[:, :nv], preferred_element_type=f32)
        h = (gate * jax.nn.sigmoid(gate) * up * 0.5).astype(bf16)
        acc_ref[...] += jnp.dot(h, wd_ref[:nv, :], preferred_element_type=f32)

    if f_last == tf:
        accumulate(tf)
    else:
        pl.when(j < last)(functools.partial(accumulate, tf))
        pl.when(j == last)(functools.partial(accumulate, f_last))

    if final_norm or n_out > 1:
        @pl.when(j == last)
        def _():
            for p in range(n_out):
                def finish(o_ref=o_refs[p]):
                    y = acc_ref[...]
                    o_ref[...] = _rms(y, fg_ref[...]) if final_norm else y
                in_part(out_tiles, p, finish)


def _ffn(xs, g, wg, wu, wd, fg, *, layer, final_norm, tm, tf, out_rows):
    D = xs[0].shape[1]
    T = sum(x.shape[0] for x in xs)
    F = wg.shape[2]
    n_f = pl.cdiv(F, tf)
    in_tiles = _part_tiles([x.shape[0] for x in xs], tm)
    out_tiles = _part_tiles(out_rows, tm)
    assert sum(out_rows) == T

    def part_spec(lo, hi):
        return pl.BlockSpec((tm, D), lambda i, j: (jnp.clip(i - lo, 0, hi - lo - 1), 0))

    outs = pl.pallas_call(
        functools.partial(_ffn_body, in_tiles=in_tiles, out_tiles=out_tiles, final_norm=final_norm,
                          f_last=F - (n_f - 1) * tf),
        grid=(T // tm, n_f),
        in_specs=[part_spec(lo, hi) for lo, hi in in_tiles] + [
            pl.BlockSpec((1, D), lambda i, j: (0, 0)),
            pl.BlockSpec((None, D, tf), lambda i, j: (layer, 0, j)),
            pl.BlockSpec((None, D, tf), lambda i, j: (layer, 0, j)),
            pl.BlockSpec((None, tf, D), lambda i, j: (layer, j, 0)),
            pl.BlockSpec((1, D), lambda i, j: (0, 0)),
        ],
        out_specs=[part_spec(lo, hi) for lo, hi in out_tiles],
        out_shape=[jax.ShapeDtypeStruct((r, D), f32) for r in out_rows],
        scratch_shapes=[pltpu.VMEM((tm, D), bf16)] + ([pltpu.VMEM((tm, D), f32)] if len(out_rows) > 1 else []),
        compiler_params=pltpu.CompilerParams(
            dimension_semantics=("arbitrary", "arbitrary"), vmem_limit_bytes=VMEM_LIMIT),
        name="ffn",
    )(*xs, g, wg, wu, wd, fg)
    return outs


def _rope(x, cos, sin_signed):
    return x * cos + pltpu.roll(x, HEAD_DIM // 2, 1) * sin_signed


def _inproj_body(x_ref, g_ref, wa_ref, wb_ref, wgate_ref, cos_ref, sin_ref, z_ref, gates_ref, xn_ref, *, n_a,
                 q_tiles, k_tiles, q_scale):
    j = pl.program_id(1)

    @pl.when(j == 0)
    def _():
        xn = _rms(x_ref[...], g_ref[...]).astype(bf16)
        xn_ref[...] = xn
        gates_ref[...] = jnp.dot(xn, wgate_ref[...], preferred_element_type=f32)

    is_q = (j >= q_tiles[0]) & (j < q_tiles[1])
    is_k = (j >= k_tiles[0]) & (j < k_tiles[1])
    rotary = is_q | is_k

    @pl.when(j < n_a)
    def _():
        z_ref[...] = jnp.dot(xn_ref[...], wa_ref[...], preferred_element_type=f32).astype(z_ref.dtype)

    @pl.when((j >= n_a) & jnp.logical_not(rotary))
    def _():
        z_ref[...] = jnp.dot(xn_ref[...], wb_ref[...], preferred_element_type=f32).astype(z_ref.dtype)

    @pl.when(rotary)
    def _():
        acc = jnp.dot(xn_ref[...], wb_ref[...], preferred_element_type=f32)
        sc = jnp.where(is_q, q_scale, 1.0)
        cos = cos_ref[...] * sc
        sin = sin_ref[...] * sc
        for c in range(acc.shape[1] // HEAD_DIM):
            cols = slice(c * HEAD_DIM, (c + 1) * HEAD_DIM)
            z_ref[:, cols] = _rope(acc[:, cols], cos, sin).astype(z_ref.dtype)


def _inproj(x, g, wa, wb, wgate, cos, sin_signed, *, layer, tm, tn, S, q_cols, k_cols, q_scale):
    T, D = x.shape
    Na, Nb = wa.shape[2], wb.shape[2]
    n_pos = S // tm
    n_a = Na // tn
    tiles = lambda cols: (cols[0] // tn, cols[1] // tn)
    assert all(c % tn == 0 for c in q_cols + k_cols + (Na, Nb)) and q_cols[0] >= Na and k_cols[0] >= Na and S % tm == 0
    return pl.pallas_call(
        functools.partial(_inproj_body, n_a=n_a, q_tiles=tiles(q_cols), k_tiles=tiles(k_cols), q_scale=q_scale),
        grid=(T // tm, (Na + Nb) // tn),
        in_specs=[
            pl.BlockSpec((tm, D), lambda i, j: (i, 0)),
            pl.BlockSpec((1, D), lambda i, j: (0, 0)),
            pl.BlockSpec((None, D, tn), lambda i, j: (layer, 0, jnp.minimum(j, n_a - 1))),
            pl.BlockSpec((None, D, tn), lambda i, j: (layer, 0, jnp.maximum(j - n_a, 0))),
            pl.BlockSpec((None, D, LANES), lambda i, j: (layer, 0, 0)),
            pl.BlockSpec((tm, HEAD_DIM), lambda i, j: (i % n_pos, 0)),
            pl.BlockSpec((tm, HEAD_DIM), lambda i, j: (i % n_pos, 0)),
        ],
        out_specs=[
            pl.BlockSpec((tm, tn), lambda i, j: (i, j)),
            pl.BlockSpec((tm, LANES), lambda i, j: (i, 0)),
        ],
        out_shape=[jax.ShapeDtypeStruct((T, Na + Nb), bf16), jax.ShapeDtypeStruct((T, LANES), f32)],
        scratch_shapes=[pltpu.VMEM((tm, D), bf16)],
        compiler_params=pltpu.CompilerParams(
            dimension_semantics=("parallel", "arbitrary"), vmem_limit_bytes=VMEM_LIMIT),
        name="inproj",
    )(x, g, wa, wb, wgate, cos, sin_signed)


def _log_sigmoid(x):
    return jnp.minimum(x, 0.0) - jnp.log(1.0 + jnp.exp(-jnp.abs(x)))


def _mlstm_body(qp_ref, kp_ref, v_ref, og_ref, gt_ref, bias_ref, cwq_ref, cwk_ref, gain_ref, out_ref,
                q_s, kt_s, rows_s, cm_s, bb_s, c_s, h_s, *, n_hp):
    S = qp_ref.shape[0]
    L = CHUNK
    nc = S // L
    d = HEAD_DIM

    def conv_silu(x, w):
        row = lax.broadcasted_iota(jnp.int32, x.shape, 0)
        prev = jnp.where(row == 0, 0.0, pltpu.roll(x, 1, 0))
        nxt = jnp.where(row == S - 1, 0.0, pltpu.roll(x, S - 1, 0))
        y = prev * w[0:1, :] + x * w[1:2, :] + nxt * w[2:3, :]
        return y * jax.nn.sigmoid(y)

    r_i = lax.broadcasted_iota(jnp.int32, (L, L), 0)
    c_i = lax.broadcasted_iota(jnp.int32, (L, L), 1)
    lower = c_i <= r_i
    upper = c_i >= r_i
    eye = c_i == r_i

    for hh in range(n_hp):
        cols = slice(hh * d, (hh + 1) * d)
        q_s[:, cols] = conv_silu(qp_ref[:, cols].astype(f32), cwq_ref[:, cols]).astype(bf16)
        k_all = conv_silu(kp_ref[:, cols].astype(f32), cwk_ref[:, cols]) * (d ** -0.5)
        for c in range(nc):
            kt_s[hh, c] = k_all[c * L:(c + 1) * L, :].T.astype(bf16)

        bias = bias_ref[hh]
        ig_f = gt_ref[hh, 0] + bias[0:1, :]
        lf_f = _log_sigmoid(gt_ref[hh, 1] + bias[1:2, :])
        ig_b = gt_ref[hh, 2] + bias[2:3, :]
        lf_b = _log_sigmoid(gt_ref[hh, 3] + bias[3:4, :])
        b_f = jnp.dot(lf_f, upper.astype(f32), preferred_element_type=f32, precision=lax.Precision.HIGHEST)
        b_b = jnp.dot(lf_b, lower.astype(f32), preferred_element_type=f32, precision=lax.Precision.HIGHEST)
        rows_s[hh, 0] = b_f
        rows_s[hh, 1] = ig_f - b_f
        rows_s[hh, 2] = b_b
        rows_s[hh, 3] = ig_b - b_b

    sides = ((0, lower, L - 1), (1, upper, 0))

    def prep(c, carry):
        r0 = pl.multiple_of(c * L, L)
        for hh in range(n_hp):
            for dr, visible, _ in sides:
                b_row = rows_s[hh, 2 * dr, pl.ds(c, 1), :]
                u_row = rows_s[hh, 2 * dr + 1, pl.ds(c, 1), :]
                dm = jnp.where(visible, jnp.broadcast_to(u_row, (L, L)), -jnp.inf)
                cm = jnp.max(dm, axis=1, keepdims=True)
                cm_s[hh, dr, pl.ds(r0, L), :] = jnp.broadcast_to(cm, (L, LANES))
                b_col = jnp.sum(jnp.where(eye, jnp.broadcast_to(b_row, (L, L)), 0.0), axis=1, keepdims=True)
                bb_s[hh, dr, pl.ds(r0, L), :] = jnp.broadcast_to(b_col, (L, LANES))
        return carry

    lax.fori_loop(0, nc, prep, 0)

    c_s[...] = jnp.zeros_like(c_s)
    ones = jnp.ones((L, LANES), bf16)
    twice = lambda a: jnp.concatenate([a, a], axis=1)

    def step(t, ms):
        chains = [(hh, dr, visible, full_row) for hh in range(n_hp) for dr, visible, full_row in sides]
        first, second, m_out = [], [], []
        for (hh, dr, visible, full_row), m in zip(chains, ms):
            cols = slice(hh * d, (hh + 1) * d)
            c = (nc - 1 - t) if dr else t
            r0 = pl.multiple_of(c * L, L)
            u_row = rows_s[hh, 2 * dr + 1, pl.ds(c, 1), :]
            big_m = jnp.maximum(m, cm_s[hh, dr, pl.ds(r0, L), :])
            dm = jnp.where(visible, jnp.broadcast_to(u_row, (L, L)), -jnp.inf)
            w = jnp.exp(dm - big_m)
            floor = jnp.exp(-(bb_s[hh, dr, pl.ds(r0, L), :] + big_m))
            w_inter = jnp.exp(m - big_m)
            m_full = big_m[full_row:full_row + 1, :]
            ws = jnp.exp(u_row - m_full)
            wc = jnp.exp(m - m_full)
            m_out.append(bb_s[hh, dr, pl.ds(r0 + full_row, 1), :] + m_full)

            q = q_s[pl.ds(r0, L), cols]
            kt = kt_s[hh, c]
            v_aug = jnp.concatenate([v_ref[pl.ds(r0, L), cols], ones], axis=1)
            state = c_s[hh, dr]
            s = jnp.dot(q, kt, preferred_element_type=f32)
            inter = jnp.dot(q, state.astype(bf16), preferred_element_type=f32)
            upd = jnp.dot((kt.astype(f32) * ws).astype(bf16), v_aug, preferred_element_type=f32)
            first.append((s, w, v_aug, twice(w_inter) * inter, floor, twice(wc) * state + upd))
        for s, w, v_aug, inter, floor, state in first:
            tot = jnp.dot((s * w).astype(bf16), v_aug, preferred_element_type=f32) + inter
            second.append((tot[:, :d] / jnp.maximum(jnp.abs(tot[:, d:]), floor), state))
        for (hh, dr, _, _), (h, state) in zip(chains, second):
            c = (nc - 1 - t) if dr else t
            h_s[dr, pl.ds(pl.multiple_of(c * L, L), L), hh * d:(hh + 1) * d] = h
            c_s[hh, dr] = state
        return tuple(m_out)

    zero = jnp.zeros((1, LANES), f32)
    lax.fori_loop(0, nc, step, (zero,) * (2 * n_hp))

    for hh in range(n_hp):
        cols = slice(hh * d, (hh + 1) * d)
        hm = _rms(h_s[0, :, cols] + h_s[1, :, cols], gain_ref[:, cols])
        out_ref[:, cols] = (jax.nn.sigmoid(og_ref[:, cols].astype(f32)) * hm).astype(out_ref.dtype)


def _mlstm(z, gt, bias, conv_w, gain, *, n_seq, S, n_heads, n_hp):
    T = z.shape[0]
    d = HEAD_DIM
    nc = S // CHUNK
    G = n_heads // n_hp
    w = n_hp * d
    seq_col = lambda off: pl.BlockSpec((S, w), lambda b, g: (b, off + g))
    return pl.pallas_call(
        functools.partial(_mlstm_body, n_hp=n_hp),
        grid=(n_seq, G),
        in_specs=[
            seq_col(0), seq_col(G), seq_col(2 * G), seq_col(3 * G),
            pl.BlockSpec((None, n_hp, N_GATE_KINDS, nc, CHUNK), lambda b, g: (b, g, 0, 0, 0)),
            pl.BlockSpec((n_hp, N_GATE_KINDS, LANES), lambda b, g: (g, 0, 0)),
            pl.BlockSpec((CONV_W, w), lambda b, g: (0, g)),
            pl.BlockSpec((CONV_W, w), lambda b, g: (0, G + g)),
            pl.BlockSpec((1, w), lambda b, g: (0, g)),
        ],
        out_specs=pl.BlockSpec((S, w), lambda b, g: (b, g)),
        out_shape=jax.ShapeDtypeStruct((T, n_heads * d), bf16),
        scratch_shapes=[
            pltpu.VMEM((S, w), bf16),
            pltpu.VMEM((n_hp, nc, d, CHUNK), bf16),
            pltpu.VMEM((n_hp, N_GATE_KINDS, nc, CHUNK), f32),
            pltpu.VMEM((n_hp, 2, S, LANES), f32),
            pltpu.VMEM((n_hp, 2, S, LANES), f32),
            pltpu.VMEM((n_hp, 2, d, 2 * LANES), f32),
            pltpu.VMEM((2, S, w), f32),
        ],
        compiler_params=pltpu.CompilerParams(
            dimension_semantics=("parallel", "parallel"), vmem_limit_bytes=VMEM_LIMIT),
        name="mlstm",
    )(z, z, z, z, gt, bias, conv_w, conv_w, gain)


def _attn_body(q_ref, k_ref, v_ref, lq1_ref, lk1_ref, lq2_ref, lk2_ref, g_ref, out_ref, *, lambda_init, tk):
    d = HEAD_DIM
    S = k_ref.shape[0]
    nt = (((1,), (1,)), ((), ()))

    def component(q, col):
        m = l = acc = None
        for c in range(S // tk):
            rows = slice(c * tk, (c + 1) * tk)
            s = lax.dot_general(q, k_ref[rows, col:col + d], nt, preferred_element_type=f32)
            m_c = jnp.max(s, axis=1, keepdims=True)
            if c == 0:
                m = m_c
                p = jnp.exp2(s - m)
                l = jnp.sum(p, axis=1, keepdims=True)
                acc = jnp.dot(p.astype(bf16), v_ref[rows, :], preferred_element_type=f32)
            else:
                m_new = jnp.maximum(m, m_c)
                alpha = jnp.exp2(m - m_new)
                p = jnp.exp2(s - m_new)
                l = alpha * l + jnp.sum(p, axis=1, keepdims=True)
                acc = alpha * acc + jnp.dot(p.astype(bf16), v_ref[rows, :], preferred_element_type=f32)
                m = m_new
        return acc, l

    acc1, l1 = component(q_ref[:, :d], 0)
    acc2, l2 = component(q_ref[:, d:], d)
    lam = (jnp.exp(jnp.sum(lq1_ref[...] * lk1_ref[...], axis=1, keepdims=True))
           - jnp.exp(jnp.sum(lq2_ref[...] * lk2_ref[...], axis=1, keepdims=True)) + lambda_init)
    o = acc1 * (1.0 / l1) - acc2 * (lam / l2)
    o = _rms(o, g_ref[...]) * (1.0 - lambda_init)
    out_ref[...] = o.astype(out_ref.dtype)


def _attn(z, lq1, lk1, lq2, lk2, g, *, n_seq, S, n_heads, col0, lambda_init, tq, tk):
    T = z.shape[0]
    d = HEAD_DIM
    nq = S // tq
    A = n_heads
    row = lambda b, h, i: (0, 0)
    return pl.pallas_call(
        functools.partial(_attn_body, lambda_init=lambda_init, tk=tk),
        grid=(n_seq, A, nq),
        in_specs=[
            pl.BlockSpec((tq, 2 * d), lambda b, h, i: (b * nq + i, col0 + h)),
            pl.BlockSpec((S, 2 * d), lambda b, h, i: (b, col0 + A + h)),
            pl.BlockSpec((S, 2 * d), lambda b, h, i: (b, col0 + 2 * A + h)),
            pl.BlockSpec((1, d), row), pl.BlockSpec((1, d), row),
            pl.BlockSpec((1, d), row), pl.BlockSpec((1, d), row),
            pl.BlockSpec((1, 2 * d), row),
        ],
        out_specs=pl.BlockSpec((tq, 2 * d), lambda b, h, i: (b * nq + i, h)),
        out_shape=jax.ShapeDtypeStruct((T, A * 2 * d), bf16),
        compiler_params=pltpu.CompilerParams(
            dimension_semantics=("parallel", "parallel", "parallel"), vmem_limit_bytes=VMEM_LIMIT),
        name="diffattn",
    )(z, z, z, lq1, lk1, lq2, lk2, g)


def _outproj_body(x_ref, om_ref, oa_ref, wm_ref, wa_ref, o_ref):
    o_ref[...] = (x_ref[...] + jnp.dot(om_ref[...], wm_ref[...], preferred_element_type=f32)
                  + jnp.dot(oa_ref[...], wa_ref[...], preferred_element_type=f32))


def _outproj(x, om, oa, w, *, layer, tm):
    T, D = x.shape
    Wm, Wa = om.shape[1], oa.shape[1]
    assert Wm == Wa and Wm + Wa == D
    return pl.pallas_call(
        _outproj_body,
        grid=(T // tm,),
        in_specs=[
            pl.BlockSpec((tm, D), lambda i: (i, 0)),
            pl.BlockSpec((tm, Wm), lambda i: (i, 0)),
            pl.BlockSpec((tm, Wa), lambda i: (i, 0)),
            pl.BlockSpec((None, Wm, D), lambda i: (layer, 0, 0)),
            pl.BlockSpec((None, Wa, D), lambda i: (layer, 1, 0)),
        ],
        out_specs=pl.BlockSpec((tm, D), lambda i: (i, 0)),
        out_shape=jax.ShapeDtypeStruct((T, D), f32),
        compiler_params=pltpu.CompilerParams(
            dimension_semantics=("parallel",), vmem_limit_bytes=VMEM_LIMIT),
        name="outproj",
    )(x, om, oa, w, w)


def _tiles(T, S):
    g = math.gcd(T, S)
    return math.gcd(512, g), math.gcd(512, g), math.gcd(1024, g), math.gcd(512, S), math.gcd(1024, S)


def kernel(x_prompt, x_sample, ffn1_norm, ffn1_w_gate, ffn1_w_up, ffn1_w_down, mix_norm, w_in, conv_qk, b_gate,
           mlstm_norm, lambda_q1, lambda_k1, lambda_q2, lambda_k2, diff_subln, w_out, ffn2_norm, ffn2_w_gate,
           ffn2_w_up, ffn2_w_down, final_norm):
    depth, D, F = ffn1_w_gate.shape
    S = x_prompt.shape[1]
    assert x_sample.shape[1] == S and S % CHUNK == 0
    d = HEAD_DIM
    m_w = D // 2
    MH = m_w // d
    a_w = D - m_w
    AH = a_w // (2 * d)
    n_gates = N_GATE_KINDS * MH
    assert n_gates <= LANES and w_in.shape[2] == 4 * m_w + n_gates + 3 * a_w
    off_g = 4 * m_w
    n_seq = x_prompt.shape[0] + x_sample.shape[0]
    T = n_seq * S
    tm, tm_ffn, tm_in, tq, tk = _tiles(T, S)
    tf = min(512, _round_up(F, LANES))
    tn = math.gcd(1024, math.gcd(4 * m_w, a_w))
    nc = S // CHUNK
    n_p = x_prompt.shape[0] * S

    inv = 1.0 / (ROPE_THETA ** (jnp.arange(0, d, 2, dtype=f32) / d))
    ang = jnp.arange(S, dtype=f32)[:, None] * inv[None, :]
    emb = jnp.concatenate([ang, ang], axis=-1)
    cos = jnp.cos(emb)
    sin_signed = jnp.sin(emb) * jnp.concatenate([-jnp.ones((d // 2,), f32), jnp.ones((d // 2,), f32)])

    ffn_w = [tuple(w.astype(bf16) for w in ws) for ws in
             ((ffn1_w_gate, ffn1_w_up, ffn1_w_down), (ffn2_w_gate, ffn2_w_up, ffn2_w_down))]
    ffn_g = (ffn1_norm, ffn2_norm)
    w_a = w_in[:, :, :off_g].astype(bf16)
    w_b = w_in[:, :, off_g + n_gates:].astype(bf16)
    w_gate = w_in[:, :, off_g:off_g + n_gates].reshape(depth, D, N_GATE_KINDS, MH).transpose(0, 1, 3, 2)
    w_gate = jnp.pad(w_gate.reshape(depth, D, n_gates), ((0, 0), (0, 0), (0, LANES - n_gates))).astype(bf16)
    w_o = w_out.astype(bf16)

    row = lambda v: v.reshape(1, -1).astype(f32)

    def ffn(xs, which, l, out_rows):
        wg, wu, wd = ffn_w[which]
        return _ffn(xs, row(ffn_g[which][l]), wg, wu, wd, row(final_norm), layer=l,
                    final_norm=(which == 1 and l == depth - 1), tm=tm_ffn, tf=tf, out_rows=out_rows)

    xs = (x_prompt.reshape(-1, D), x_sample.reshape(-1, D))
    for l in range(depth):
        lambda_init = 0.8 - 0.6 * math.exp(-0.3 * l)
        (x,) = ffn(xs, 0, l, (T,))
        z, gates = _inproj(x, row(mix_norm[l]), w_a, w_b, w_gate, cos, sin_signed, layer=l, tm=tm_in, tn=tn, S=S,
                           q_cols=(4 * m_w, 4 * m_w + a_w), k_cols=(4 * m_w + a_w, 4 * m_w + 2 * a_w),
                           q_scale=d ** -0.5 * math.log2(math.e))

        gt = gates[:, :n_gates].reshape(n_seq, nc, CHUNK, MH, N_GATE_KINDS).transpose(0, 3, 4, 1, 2)
        bias = jnp.broadcast_to(b_gate[l].astype(f32).T[:, :, None], (MH, N_GATE_KINDS, LANES))
        om = _mlstm(z, gt, bias, conv_qk[l].astype(f32), row(mlstm_norm[l]), n_seq=n_seq, S=S, n_heads=MH,
                    n_hp=math.gcd(2, MH))
        oa = _attn(z, row(lambda_q1[l]), row(lambda_k1[l]), row(lambda_q2[l]), row(lambda_k2[l]),
                   row(diff_subln[l]), n_seq=n_seq, S=S, n_heads=AH, col0=4 * m_w // (2 * d),
                   lambda_init=lambda_init, tq=tq, tk=tk)
        x = _outproj(x, om, oa, w_o, layer=l, tm=tm)
        xs = ffn((x,), 1, l, (n_p, T - n_p) if l == depth - 1 else (T,))

    return xs[0].reshape(x_prompt.shape), xs[1].reshape(x_sample.shape)
```

```python
import functools
import math

import jax
import jax.numpy as jnp
from jax import lax
from jax.experimental import pallas as pl
from jax.experimental.pallas import tpu as pltpu

EPS = 1e-6
HEAD_DIM = 128
CHUNK = 128
CONV_W = 3
ROPE_THETA = 10000.0
N_GATE_KINDS = 4
LANES = 128
VMEM_LIMIT = 60 * 1024 * 1024

f32 = jnp.float32
bf16 = jnp.bfloat16


def _round_up(n, m):
    return (n + m - 1) // m * m


def _rms(x, g):
    ms = jnp.mean(x * x, axis=-1, keepdims=True)
    return x * lax.rsqrt(ms + EPS) * g


def _part_tiles(rows, tm):
    tiles, lo = [], 0
    for r in rows:
        assert r % tm == 0
        tiles.append((lo, lo + r // tm))
        lo += r // tm
    return tuple(tiles)


def _ffn_body(*refs, in_tiles, out_tiles, final_norm, f_last):
    n_in, n_out = len(in_tiles), len(out_tiles)
    x_refs = refs[:n_in]
    g_ref, wg_ref, wu_ref, wd_ref, fg_ref = refs[n_in:n_in + 5]
    o_refs = refs[n_in + 5:n_in + 5 + n_out]
    xn_ref = refs[n_in + 5 + n_out]
    acc_ref = o_refs[0] if n_out == 1 else refs[n_in + 6 + n_out]
    i = pl.program_id(0)
    j = pl.program_id(1)
    last = pl.num_programs(1) - 1
    tf = wg_ref.shape[1]

    def in_part(tiles, p, fn):
        if len(tiles) == 1:
            fn()
        else:
            pl.when((i >= tiles[p][0]) & (i < tiles[p][1]))(fn)

    @pl.when(j == 0)
    def _():
        for p in range(n_in):
            def start(x_ref=x_refs[p]):
                x = x_ref[...]
                xn_ref[...] = _rms(x, g_ref[...]).astype(bf16)
                acc_ref[...] = x
            in_part(in_tiles, p, start)

    def accumulate(nv):
        xn = xn_ref[...]
        gate = jnp.dot(xn, wg_ref[:, :nv], preferred_element_type=f32)
        up = jnp.dot(xn, wu_ref[:, :nv], preferred_element_type=f32)
        h = (gate * jax.nn.sigmoid(gate) * up * 0.5).astype(bf16)
        acc_ref[...] += jnp.dot(h, wd_ref[:nv, :], preferred_element_type=f32)

    if f_last == tf:
        accumulate(tf)
    else:
        pl.when(j < last)(functools.partial(accumulate, tf))
        pl.when(j == last)(functools.partial(accumulate, f_last))

    if final_norm or n_out > 1:
        @pl.when(j == last)
        def _():
            for p in range(n_out):
                def finish(o_ref=o_refs[p]):
                    y = acc_ref[...]
                    o_ref[...] = _rms(y, fg_ref[...]) if final_norm else y
                in_part(out_tiles, p, finish)


def _ffn(xs, g, wg, wu, wd, fg, *, layer, final_norm, tm, tf, out_rows):
    D = xs[0].shape[1]
    T = sum(x.shape[0] for x in xs)
    F = wg.shape[2]
    n_f = pl.cdiv(F, tf)
    in_tiles = _part_tiles([x.shape[0] for x in xs], tm)
    out_tiles = _part_tiles(out_rows, tm)
    assert sum(out_rows) == T

    def part_spec(lo, hi):
        return pl.BlockSpec((tm, D), lambda i, j: (jnp.clip(i - lo, 0, hi - lo - 1), 0))

    outs = pl.pallas_call(
        functools.partial(_ffn_body, in_tiles=in_tiles, out_tiles=out_tiles, final_norm=final_norm,
                          f_last=F - (n_f - 1) * tf),
        grid=(T // tm, n_f),
        in_specs=[part_spec(lo, hi) for lo, hi in in_tiles] + [
            pl.BlockSpec((1, D), lambda i, j: (0, 0)),
            pl.BlockSpec((None, D, tf), lambda i, j: (layer, 0, j)),
            pl.BlockSpec((None, D, tf), lambda i, j: (layer, 0, j)),
            pl.BlockSpec((None, tf, D), lambda i, j: (layer, j, 0)),
            pl.BlockSpec((1, D), lambda i, j: (0, 0)),
        ],
        out_specs=[part_spec(lo, hi) for lo, hi in out_tiles],
        out_shape=[jax.ShapeDtypeStruct((r, D), f32) for r in out_rows],
        scratch_shapes=[pltpu.VMEM((tm, D), bf16)] + ([pltpu.VMEM((tm, D), f32)] if len(out_rows) > 1 else []),
        compiler_params=pltpu.CompilerParams(
            dimension_semantics=("arbitrary", "arbitrary"), vmem_limit_bytes=VMEM_LIMIT),
        name="ffn",
    )(*xs, g, wg, wu, wd, fg)
    return outs


def _rope(x, cos, sin_signed):
    return x * cos + pltpu.roll(x, HEAD_DIM // 2, 1) * sin_signed


def _inproj_body(x_ref, g_ref, wa_ref, wb_ref, wgate_ref, cos_ref, sin_ref, z_ref, gates_ref, xn_ref, *, n_a,
                 q_tiles, k_tiles, q_scale):
    j = pl.program_id(1)

    @pl.when(j == 0)
    def _():
        xn = _rms(x_ref[...], g_ref[...]).astype(bf16)
        xn_ref[...] = xn
        gates_ref[...] = jnp.dot(xn, wgate_ref[...], preferred_element_type=f32)

    is_q = (j >= q_tiles[0]) & (j < q_tiles[1])
    is_k = (j >= k_tiles[0]) & (j < k_tiles[1])
    rotary = is_q | is_k

    @pl.when(j < n_a)
    def _():
        z_ref[...] = jnp.dot(xn_ref[...], wa_ref[...], preferred_element_type=f32).astype(z_ref.dtype)

    @pl.when((j >= n_a) & jnp.logical_not(rotary))
    def _():
        z_ref[...] = jnp.dot(xn_ref[...], wb_ref[...], preferred_element_type=f32).astype(z_ref.dtype)

    @pl.when(rotary)
    def _():
        acc = jnp.dot(xn_ref[...], wb_ref[...], preferred_element_type=f32)
        sc = jnp.where(is_q, q_scale, 1.0)
        cos = cos_ref[...] * sc
        sin = sin_ref[...] * sc
        for c in range(acc.shape[1] // HEAD_DIM):
            cols = slice(c * HEAD_DIM, (c + 1) * HEAD_DIM)
            z_ref[:, cols] = _rope(acc[:, cols], cos, sin).astype(z_ref.dtype)


def _inproj(x, g, wa, wb, wgate, cos, sin_signed, *, layer, tm, tn, S, q_cols, k_cols, q_scale):
    T, D = x.shape
    Na, Nb = wa.shape[2], wb.shape[2]
    n_pos = S // tm
    n_a = Na // tn
    tiles = lambda cols: (cols[0] // tn, cols[1] // tn)
    assert all(c % tn == 0 for c in q_cols + k_cols + (Na, Nb)) and q_cols[0] >= Na and k_cols[0] >= Na and S % tm == 0
    return pl.pallas_call(
        functools.partial(_inproj_body, n_a=n_a, q_tiles=tiles(q_cols), k_tiles=tiles(k_cols), q_scale=q_scale),
        grid=(T // tm, (Na + Nb) // tn),
        in_specs=[
            pl.BlockSpec((tm, D), lambda i, j: (i, 0)),
            pl.BlockSpec((1, D), lambda i, j: (0, 0)),
            pl.BlockSpec((None, D, tn), lambda i, j: (layer, 0, jnp.minimum(j, n_a - 1))),
            pl.BlockSpec((None, D, tn), lambda i, j: (layer, 0, jnp.maximum(j - n_a, 0))),
            pl.BlockSpec((None, D, LANES), lambda i, j: (layer, 0, 0)),
            pl.BlockSpec((tm, HEAD_DIM), lambda i, j: (i % n_pos, 0)),
            pl.BlockSpec((tm, HEAD_DIM), lambda i, j: (i % n_pos, 0)),
        ],
        out_specs=[
            pl.BlockSpec((tm, tn), lambda i, j: (i, j)),
            pl.BlockSpec((tm, LANES), lambda i, j: (i, 0)),
        ],
        out_shape=[jax.ShapeDtypeStruct((T, Na + Nb), bf16), jax.ShapeDtypeStruct((T, LANES), f32)],
        scratch_shapes=[pltpu.VMEM((tm, D), bf16)],
        compiler_params=pltpu.CompilerParams(
            dimension_semantics=("parallel", "arbitrary"), vmem_limit_bytes=VMEM_LIMIT),
        name="inproj",
    )(x, g, wa, wb, wgate, cos, sin_signed)


def _log_sigmoid(x):
    return jnp.minimum(x, 0.0) - jnp.log(1.0 + jnp.exp(-jnp.abs(x)))


def _mlstm_body(qp_ref, kp_ref, v_ref, og_ref, gt_ref, bias_ref, cwq_ref, cwk_ref, gain_ref, out_ref,
                q_s, kt_s, rows_s, cm_s, bb_s, c_s, h_s, *, n_hp):
    S = qp_ref.shape[0]
    L = CHUNK
    nc = S // L
    d = HEAD_DIM

    def conv_silu(x, w):
        row = lax.broadcasted_iota(jnp.int32, x.shape, 0)
        prev = jnp.where(row == 0, 0.0, pltpu.roll(x, 1, 0))
        nxt = jnp.where(row == S - 1, 0.0, pltpu.roll(x, S - 1, 0))
        y = prev * w[0:1, :] + x * w[1:2, :] + nxt * w[2:3, :]
        return y * jax.nn.sigmoid(y)

    r_i = lax.broadcasted_iota(jnp.int32, (L, L), 0)
    c_i = lax.broadcasted_iota(jnp.int32, (L, L), 1)
    lower = c_i <= r_i
    upper = c_i >= r_i
    eye = c_i == r_i

    for hh in range(n_hp):
        cols = slice(hh * d, (hh + 1) * d)
        q_s[:, cols] = conv_silu(qp_ref[:, cols].astype(f32), cwq_ref[:, cols]).astype(bf16)
        k_all = conv_silu(kp_ref[:, cols].astype(f32), cwk_ref[:, cols]) * (d ** -0.5)
        for c in range(nc):
            kt_s[hh, c] = k_all[c * L:(c + 1) * L, :].T.astype(bf16)

        bias = bias_ref[hh]
        ig_f = gt_ref[hh, 0] + bias[0:1, :]
        lf_f = _log_sigmoid(gt_ref[hh, 1] + bias[1:2, :])
        ig_b = gt_ref[hh, 2] + bias[2:3, :]
        lf_b = _log_sigmoid(gt_ref[hh, 3] + bias[3:4, :])
        b_f = jnp.dot(lf_f, upper.astype(f32), preferred_element_type=f32, precision=lax.Precision.HIGHEST)
        b_b = jnp.dot(lf_b, lower.astype(f32), preferred_element_type=f32, precision=lax.Precision.HIGHEST)
        rows_s[hh, 0] = b_f
        rows_s[hh, 1] = ig_f - b_f
        rows_s[hh, 2] = b_b
        rows_s[hh, 3] = ig_b - b_b

    sides = ((0, lower, L - 1), (1, upper, 0))

    def prep(c, carry):
        r0 = pl.multiple_of(c * L, L)
        for hh in range(n_hp):
            for dr, visible, _ in sides:
                b_row = rows_s[hh, 2 * dr, pl.ds(c, 1), :]
                u_row = rows_s[hh, 2 * dr + 1, pl.ds(c, 1), :]
                dm = jnp.where(visible, jnp.broadcast_to(u_row, (L, L)), -jnp.inf)
                cm = jnp.max(dm, axis=1, keepdims=True)
                cm_s[hh, dr, pl.ds(r0, L), :] = jnp.broadcast_to(cm, (L, LANES))
                b_col = jnp.sum(jnp.where(eye, jnp.broadcast_to(b_row, (L, L)), 0.0), axis=1, keepdims=True)
                bb_s[hh, dr, pl.ds(r0, L), :] = jnp.broadcast_to(b_col, (L, LANES))
        return carry

    lax.fori_loop(0, nc, prep, 0)

    c_s[...] = jnp.zeros_like(c_s)
    ones = jnp.ones((L, LANES), bf16)
    twice = lambda a: jnp.concatenate([a, a], axis=1)

    def step(t, ms):
        chains = [(hh, dr, visible, full_row) for hh in range(n_hp) for dr, visible, full_row in sides]
        first, second, m_out = [], [], []
        for (hh, dr, visible, full_row), m in zip(chains, ms):
            cols = slice(hh * d, (hh + 1) * d)
            c = (nc - 1 - t) if dr else t
            r0 = pl.multiple_of(c * L, L)
            u_row = rows_s[hh, 2 * dr + 1, pl.ds(c, 1), :]
            big_m = jnp.maximum(m, cm_s[hh, dr, pl.ds(r0, L), :])
            dm = jnp.where(visible, jnp.broadcast_to(u_row, (L, L)), -jnp.inf)
            w = jnp.exp(dm - big_m)
            floor = jnp.exp(-(bb_s[hh, dr, pl.ds(r0, L), :] + big_m))
            w_inter = jnp.exp(m - big_m)
            m_full = big_m[full_row:full_row + 1, :]
            ws = jnp.exp(u_row - m_full)
            wc = jnp.exp(m - m_full)
            m_out.append(bb_s[hh, dr, pl.ds(r0 + full_row, 1), :] + m_full)

            q = q_s[pl.ds(r0, L), cols]
            kt = kt_s[hh, c]
            v_aug = jnp.concatenate([v_ref[pl.ds(r0, L), cols], ones], axis=1)
            state = c_s[hh, dr]
            s = jnp.dot(q, kt, preferred_element_type=f32)
            inter = jnp.dot(q, state.astype(bf16), preferred_element_type=f32)
            upd = jnp.dot((kt.astype(f32) * ws).astype(bf16), v_aug, preferred_element_type=f32)
            first.append((s, w, v_aug, twice(w_inter) * inter, floor, twice(wc) * state + upd))
        for s, w, v_aug, inter, floor, state in first:
            tot = jnp.dot((s * w).astype(bf16), v_aug, preferred_element_type=f32) + inter
            second.append((tot[:, :d] / jnp.maximum(jnp.abs(tot[:, d:]), floor), state))
        for (hh, dr, _, _), (h, state) in zip(chains, second):
            c = (nc - 1 - t) if dr else t
            h_s[dr, pl.ds(pl.multiple_of(c * L, L), L), hh * d:(hh + 1) * d] = h
            c_s[hh, dr] = state
        return tuple(m_out)

    zero = jnp.zeros((1, LANES), f32)
    lax.fori_loop(0, nc, step, (zero,) * (2 * n_hp))

    for hh in range(n_hp):
        cols = slice(hh * d, (hh + 1) * d)
        hm = _rms(h_s[0, :, cols] + h_s[1, :, cols], gain_ref[:, cols])
        out_ref[:, cols] = (jax.nn.sigmoid(og_ref[:, cols].astype(f32)) * hm).astype(out_ref.dtype)


def _mlstm(z, gt, bias, conv_w, gain, *, n_seq, S, n_heads, n_hp):
    T = z.shape[0]
    d = HEAD_DIM
    nc = S // CHUNK
    G = n_heads // n_hp
    w = n_hp * d
    seq_col = lambda off: pl.BlockSpec((S, w), lambda b, g: (b, off + g))
    return pl.pallas_call(
        functools.partial(_mlstm_body, n_hp=n_hp),
        grid=(n_seq, G),
        in_specs=[
            seq_col(0), seq_col(G), seq_col(2 * G), seq_col(3 * G),
            pl.BlockSpec((None, n_hp, N_GATE_KINDS, nc, CHUNK), lambda b, g: (b, g, 0, 0, 0)),
            pl.BlockSpec((n_hp, N_GATE_KINDS, LANES), lambda b, g: (g, 0, 0)),
            pl.BlockSpec((CONV_W, w), lambda b, g: (0, g)),
            pl.BlockSpec((CONV_W, w), lambda b, g: (0, G + g)),
            pl.BlockSpec((1, w), lambda b, g: (0, g)),
        ],
        out_specs=pl.BlockSpec((S, w), lambda b, g: (b, g)),
        out_shape=jax.ShapeDtypeStruct((T, n_heads * d), bf16),
        scratch_shapes=[
            pltpu.VMEM((S, w), bf16),
            pltpu.VMEM((n_hp, nc, d, CHUNK), bf16),
            pltpu.VMEM((n_hp, N_GATE_KINDS, nc, CHUNK), f32),
            pltpu.VMEM((n_hp, 2, S, LANES), f32),
            pltpu.VMEM((n_hp, 2, S, LANES), f32),
            pltpu.VMEM((n_hp, 2, d, 2 * LANES), f32),
            pltpu.VMEM((2, S, w), f32),
        ],
        compiler_params=pltpu.CompilerParams(
            dimension_semantics=("parallel", "parallel"), vmem_limit_bytes=VMEM_LIMIT),
        name="mlstm",
    )(z, z, z, z, gt, bias, conv_w, conv_w, gain)


def _attn_body(q_ref, k_ref, v_ref, lq1_ref, lk1_ref, lq2_ref, lk2_ref, g_ref, out_ref, *, lambda_init, tk):
    d = HEAD_DIM
    S = k_ref.shape[0]
    nt = (((1,), (1,)), ((), ()))

    def component(q, col):
        m = l = acc = None
        for c in range(S // tk):
            rows = slice(c * tk, (c + 1) * tk)
            s = lax.dot_general(q, k_ref[rows, col:col + d], nt, preferred_element_type=f32)
            m_c = jnp.max(s, axis=1, keepdims=True)
            if c == 0:
                m = m_c
                p = jnp.exp2(s - m)
                l = jnp.sum(p, axis=1, keepdims=True)
                acc = jnp.dot(p.astype(bf16), v_ref[rows, :], preferred_element_type=f32)
            else:
                m_new = jnp.maximum(m, m_c)
                alpha = jnp.exp2(m - m_new)
                p = jnp.exp2(s - m_new)
                l = alpha * l + jnp.sum(p, axis=1, keepdims=True)
                acc = alpha * acc + jnp.dot(p.astype(bf16), v_ref[rows, :], preferred_element_type=f32)
                m = m_new
        return acc, l

    acc1, l1 = component(q_ref[:, :d], 0)
    acc2, l2 = component(q_ref[:, d:], d)
    lam = (jnp.exp(jnp.sum(lq1_ref[...] * lk1_ref[...], axis=1, keepdims=True))
           - jnp.exp(jnp.sum(lq2_ref[...] * lk2_ref[...], axis=1, keepdims=True)) + lambda_init)
    o = acc1 * (1.0 / l1) - acc2 * (lam / l2)
    o = _rms(o, g_ref[...]) * (1.0 - lambda_init)
    out_ref[...] = o.astype(out_ref.dtype)


def _attn(z, lq1, lk1, lq2, lk2, g, *, n_seq, S, n_heads, col0, lambda_init, tq, tk):
    T = z.shape[0]
    d = HEAD_DIM
    nq = S // tq
    A = n_heads
    row = lambda b, h, i: (0, 0)
    return pl.pallas_call(
        functools.partial(_attn_body, lambda_init=lambda_init, tk=tk),
        grid=(n_seq, A, nq),
        in_specs=[
            pl.BlockSpec((tq, 2 * d), lambda b, h, i: (b * nq + i, col0 + h)),
            pl.BlockSpec((S, 2 * d), lambda b, h, i: (b, col0 + A + h)),
            pl.BlockSpec((S, 2 * d), lambda b, h, i: (b, col0 + 2 * A + h)),
            pl.BlockSpec((1, d), row), pl.BlockSpec((1, d), row),
            pl.BlockSpec((1, d), row), pl.BlockSpec((1, d), row),
            pl.BlockSpec((1, 2 * d), row),
        ],
        out_specs=pl.BlockSpec((tq, 2 * d), lambda b, h, i: (b * nq + i, h)),
        out_shape=jax.ShapeDtypeStruct((T, A * 2 * d), bf16),
        compiler_params=pltpu.CompilerParams(
            dimension_semantics=("parallel", "parallel", "parallel"), vmem_limit_bytes=VMEM_LIMIT),
        name="diffattn",
    )(z, z, z, lq1, lk1, lq2, lk2, g)


def _outproj_body(x_ref, om_ref, oa_ref, wm_ref, wa_ref, o_ref):
    o_ref[...] = (x_ref[...] + jnp.dot(om_ref[...], wm_ref[...], preferred_element_type=f32)
                  + jnp.dot(oa_ref[...], wa_ref[...], preferred_element_type=f32))


def _outproj(x, om, oa, w, *, layer, tm):
    T, D = x.shape
    Wm, Wa = om.shape[1], oa.shape[1]
    assert Wm == Wa and Wm + Wa == D
    return pl.pallas_call(
        _outproj_body,
        grid=(T // tm,),
        in_specs=[
            pl.BlockSpec((tm, D), lambda i: (i, 0)),
            pl.BlockSpec((tm, Wm), lambda i: (i, 0)),
            pl.BlockSpec((tm, Wa), lambda i: (i, 0)),
            pl.BlockSpec((None, Wm, D), lambda i: (layer, 0, 0)),
            pl.BlockSpec((None, Wa, D), lambda i: (layer, 1, 0)),
        ],
        out_specs=pl.BlockSpec((tm, D), lambda i: (i, 0)),
        out_shape=jax.ShapeDtypeStruct((T, D), f32),
        compiler_params=pltpu.CompilerParams(
            dimension_semantics=("parallel",), vmem_limit_bytes=VMEM_LIMIT),
        name="outproj",
    )(x, om, oa, w, w)


def _tiles(T, S):
    g = math.gcd(T, S)
    return math.gcd(512, g), math.gcd(512, g), math.gcd(1024, g), math.gcd(512, S), math.gcd(1024, S)


def kernel(x_prompt, x_sample, ffn1_norm, ffn1_w_gate, ffn1_w_up, ffn1_w_down, mix_norm, w_in, conv_qk, b_gate,
           mlstm_norm, lambda_q1, lambda_k1, lambda_q2, lambda_k2, diff_subln, w_out, ffn2_norm, ffn2_w_gate,
           ffn2_w_up, ffn2_w_down, final_norm):
    depth, D, F = ffn1_w_gate.shape
    S = x_prompt.shape[1]
    assert x_sample.shape[1] == S and S % CHUNK == 0
    d = HEAD_DIM
    m_w = D // 2
    MH = m_w // d
    a_w = D - m_w
    AH = a_w // (2 * d)
    n_gates = N_GATE_KINDS * MH
    assert n_gates <= LANES and w_in.shape[2] == 4 * m_w + n_gates + 3 * a_w
    off_g = 4 * m_w
    n_seq = x_prompt.shape[0] + x_sample.shape[0]
    T = n_seq * S
    tm, tm_ffn, tm_in, tq, tk = _tiles(T, S)
    tf = min(1024, _round_up(F, LANES))
    tn = math.gcd(1024, math.gcd(4 * m_w, a_w))
    nc = S // CHUNK
    n_p = x_prompt.shape[0] * S

    inv = 1.0 / (ROPE_THETA ** (jnp.arange(0, d, 2, dtype=f32) / d))
    ang = jnp.arange(S, dtype=f32)[:, None] * inv[None, :]
    emb = jnp.concatenate([ang, ang], axis=-1)
    cos = jnp.cos(emb)
    sin_signed = jnp.sin(emb) * jnp.concatenate([-jnp.ones((d // 2,), f32), jnp.ones((d // 2,), f32)])

    ffn_w = [tuple(w.astype(bf16) for w in ws) for ws in
             ((ffn1_w_gate, ffn1_w_up, ffn1_w_down), (ffn2_w_gate, ffn2_w_up, ffn2_w_down))]
    ffn_g = (ffn1_norm, ffn2_norm)
    w_a = w_in[:, :, :off_g].astype(bf16)
    w_b = w_in[:, :, off_g + n_gates:].astype(bf16)
    w_gate = w_in[:, :, off_g:off_g + n_gates].reshape(depth, D, N_GATE_KINDS, MH).transpose(0, 1, 3, 2)
    w_gate = jnp.pad(w_gate.reshape(depth, D, n_gates), ((0, 0), (0, 0), (0, LANES - n_gates))).astype(bf16)
    w_o = w_out.astype(bf16)

    row = lambda v: v.reshape(1, -1).astype(f32)

    def ffn(xs, which, l, out_rows):
        wg, wu, wd = ffn_w[which]
        return _ffn(xs, row(ffn_g[which][l]), wg, wu, wd, row(final_norm), layer=l,
                    final_norm=(which == 1 and l == depth - 1), tm=tm_ffn, tf=tf, out_rows=out_rows)

    xs = (x_prompt.reshape(-1, D), x_sample.reshape(-1, D))
    for l in range(depth):
        lambda_init = 0.8 - 0.6 * math.exp(-0.3 * l)
        (x,) = ffn(xs, 0, l, (T,))
        z, gates = _inproj(x, row(mix_norm[l]), w_a, w_b, w_gate, cos, sin_signed, layer=l, tm=tm_in, tn=tn, S=S,
                           q_cols=(4 * m_w, 4 * m_w + a_w), k_cols=(4 * m_w + a_w, 4 * m_w + 2 * a_w),
                           q_scale=d ** -0.5 * math.log2(math.e))

        gt = gates[:, :n_gates].reshape(n_seq, nc, CHUNK, MH, N_GATE_KINDS).transpose(0, 3, 4, 1, 2)
        bias = jnp.broadcast_to(b_gate[l].astype(f32).T[:, :, None], (MH, N_GATE_KINDS, LANES))
        om = _mlstm(z, gt, bias, conv_qk[l].astype(f32), row(mlstm_norm[l]), n_seq=n_seq, S=S, n_heads=MH,
                    n_hp=math.gcd(2, MH))
        oa = _attn(z, row(lambda_q1[l]), row(lambda_k1[l]), row(lambda_q2[l]), row(lambda_k2[l]),
                   row(diff_subln[l]), n_seq=n_seq, S=S, n_heads=AH, col0=4 * m_w // (2 * d),
                   lambda_init=lambda_init, tq=tq, tk=tk)
        x = _outproj(x, om, oa, w_o, layer=l, tm=tm)
        xs = ffn((x,), 1, l, (n_p, T - n_p) if l == depth - 1 else (T,))

    return xs[0].reshape(x_prompt.shape), xs[1].reshape(x_sample.shape)
```

```python
import functools
import math

import jax
import jax.numpy as jnp
from jax import lax
from jax.experimental import pallas as pl
from jax.experimental.pallas import tpu as pltpu

EPS = 1e-6
HEAD_DIM = 128
CHUNK = 128
CONV_W = 3
ROPE_THETA = 10000.0
N_GATE_KINDS = 4
LANES = 128
VMEM_LIMIT = 60 * 1024 * 1024

f32 = jnp.float32
bf16 = jnp.bfloat16


def _round_up(n, m):
    return (n + m - 1) // m * m


def _rms(x, g):
    ms = jnp.mean(x * x, axis=-1, keepdims=True)
    return x * lax.rsqrt(ms + EPS) * g


def _part_tiles(rows, tm):
    tiles, lo = [], 0
    for r in rows:
        assert r % tm == 0
        tiles.append((lo, lo + r // tm))
        lo += r // tm
    return tuple(tiles)


def _ffn_body(*refs, in_tiles, out_tiles, final_norm, f_last):
    n_in, n_out = len(in_tiles), len(out_tiles)
    x_refs = refs[:n_in]
    g_ref, wg_ref, wu_ref, wd_ref, fg_ref = refs[n_in:n_in + 5]
    o_refs = refs[n_in + 5:n_in + 5 + n_out]
    xn_ref = refs[n_in + 5 + n_out]
    acc_ref = o_refs[0] if n_out == 1 else refs[n_in + 6 + n_out]
    i = pl.program_id(0)
    j = pl.program_id(1)
    last = pl.num_programs(1) - 1
    tf = wg_ref.shape[1]

    def in_part(tiles, p, fn):
        if len(tiles) == 1:
            fn()
        else:
            pl.when((i >= tiles[p][0]) & (i < tiles[p][1]))(fn)

    @pl.when(j == 0)
    def _():
        for p in range(n_in):
            def start(x_ref=x_refs[p]):
                x = x_ref[...]
                xn_ref[...] = _rms(x, g_ref[...]).astype(bf16)
                acc_ref[...] = x
            in_part(in_tiles, p, start)

    def accumulate(nv):
        xn = xn_ref[...]
        gate = jnp.dot(xn, wg_ref[:, :nv], preferred_element_type=f32)
        up = jnp.dot(xn, wu_ref[:, :nv], preferred_element_type=f32)
        h = (gate * jax.nn.sigmoid(gate) * up * 0.5).astype(bf16)
        acc_ref[...] += jnp.dot(h, wd_ref[:nv, :], preferred_element_type=f32)

    if f_last == tf:
        accumulate(tf)
    else:
        pl.when(j < last)(functools.partial(accumulate, tf))
        pl.when(j == last)(functools.partial(accumulate, f_last))

    if final_norm or n_out > 1:
        @pl.when(j == last)
        def _():
            for p in range(n_out):
                def finish(o_ref=o_refs[p]):
                    y = acc_ref[...]
                    o_ref[...] = _rms(y, fg_ref[...]) if final_norm else y
                in_part(out_tiles, p, finish)


def _ffn(xs, g, wg, wu, wd, fg, *, layer, final_norm, tm, tf, out_rows):
    D = xs[0].shape[1]
    T = sum(x.shape[0] for x in xs)
    F = wg.shape[2]
    n_f = pl.cdiv(F, tf)
    in_tiles = _part_tiles([x.shape[0] for x in xs], tm)
    out_tiles = _part_tiles(out_rows, tm)
    assert sum(out_rows) == T

    def part_spec(lo, hi):
        return pl.BlockSpec((tm, D), lambda i, j: (jnp.clip(i - lo, 0, hi - lo - 1), 0))

    outs = pl.pallas_call(
        functools.partial(_ffn_body, in_tiles=in_tiles, out_tiles=out_tiles, final_norm=final_norm,
                          f_last=F - (n_f - 1) * tf),
        grid=(T // tm, n_f),
        in_specs=[part_spec(lo, hi) for lo, hi in in_tiles] + [
            pl.BlockSpec((1, D), lambda i, j: (0, 0)),
            pl.BlockSpec((None, D, tf), lambda i, j: (layer, 0, j)),
            pl.BlockSpec((None, D, tf), lambda i, j: (layer, 0, j)),
            pl.BlockSpec((None, tf, D), lambda i, j: (layer, j, 0)),
            pl.BlockSpec((1, D), lambda i, j: (0, 0)),
        ],
        out_specs=[part_spec(lo, hi) for lo, hi in out_tiles],
        out_shape=[jax.ShapeDtypeStruct((r, D), f32) for r in out_rows],
        scratch_shapes=[pltpu.VMEM((tm, D), bf16)] + ([pltpu.VMEM((tm, D), f32)] if len(out_rows) > 1 else []),
        compiler_params=pltpu.CompilerParams(
            dimension_semantics=("arbitrary", "arbitrary"), vmem_limit_bytes=VMEM_LIMIT),
        name="ffn",
    )(*xs, g, wg, wu, wd, fg)
    return outs


def _rope(x, cos, sin_signed):
    return x * cos + pltpu.roll(x, HEAD_DIM // 2, 1) * sin_signed


def _inproj_body(x_ref, g_ref, wa_ref, wb_ref, wgate_ref, cos_ref, sin_ref, z_ref, gates_ref, xn_ref, *, n_a,
                 q_tiles, k_tiles, q_scale):
    j = pl.program_id(1)

    @pl.when(j == 0)
    def _():
        xn = _rms(x_ref[...], g_ref[...]).astype(bf16)
        xn_ref[...] = xn
        gates_ref[...] = jnp.dot(xn, wgate_ref[...], preferred_element_type=f32)

    is_q = (j >= q_tiles[0]) & (j < q_tiles[1])
    is_k = (j >= k_tiles[0]) & (j < k_tiles[1])
    rotary = is_q | is_k

    @pl.when(j < n_a)
    def _():
        z_ref[...] = jnp.dot(xn_ref[...], wa_ref[...], preferred_element_type=f32).astype(z_ref.dtype)

    @pl.when((j >= n_a) & jnp.logical_not(rotary))
    def _():
        z_ref[...] = jnp.dot(xn_ref[...], wb_ref[...], preferred_element_type=f32).astype(z_ref.dtype)

    @pl.when(rotary)
    def _():
        acc = jnp.dot(xn_ref[...], wb_ref[...], preferred_element_type=f32)
        sc = jnp.where(is_q, q_scale, 1.0)
        cos = cos_ref[...] * sc
        sin = sin_ref[...] * sc
        for c in range(acc.shape[1] // HEAD_DIM):
            cols = slice(c * HEAD_DIM, (c + 1) * HEAD_DIM)
            z_ref[:, cols] = _rope(acc[:, cols], cos, sin).astype(z_ref.dtype)


def _inproj(x, g, wa, wb, wgate, cos, sin_signed, *, layer, tm, tn, S, q_cols, k_cols, q_scale):
    T, D = x.shape
    Na, Nb = wa.shape[2], wb.shape[2]
    n_pos = S // tm
    n_a = Na // tn
    tiles = lambda cols: (cols[0] // tn, cols[1] // tn)
    assert all(c % tn == 0 for c in q_cols + k_cols + (Na, Nb)) and q_cols[0] >= Na and k_cols[0] >= Na and S % tm == 0
    return pl.pallas_call(
        functools.partial(_inproj_body, n_a=n_a, q_tiles=tiles(q_cols), k_tiles=tiles(k_cols), q_scale=q_scale),
        grid=(T // tm, (Na + Nb) // tn),
        in_specs=[
            pl.BlockSpec((tm, D), lambda i, j: (i, 0)),
            pl.BlockSpec((1, D), lambda i, j: (0, 0)),
            pl.BlockSpec((None, D, tn), lambda i, j: (layer, 0, jnp.minimum(j, n_a - 1))),
            pl.BlockSpec((None, D, tn), lambda i, j: (layer, 0, jnp.maximum(j - n_a, 0))),
            pl.BlockSpec((None, D, LANES), lambda i, j: (layer, 0, 0)),
            pl.BlockSpec((tm, HEAD_DIM), lambda i, j: (i % n_pos, 0)),
            pl.BlockSpec((tm, HEAD_DIM), lambda i, j: (i % n_pos, 0)),
        ],
        out_specs=[
            pl.BlockSpec((tm, tn), lambda i, j: (i, j)),
            pl.BlockSpec((tm, LANES), lambda i, j: (i, 0)),
        ],
        out_shape=[jax.ShapeDtypeStruct((T, Na + Nb), bf16), jax.ShapeDtypeStruct((T, LANES), f32)],
        scratch_shapes=[pltpu.VMEM((tm, D), bf16)],
        compiler_params=pltpu.CompilerParams(
            dimension_semantics=("parallel", "arbitrary"), vmem_limit_bytes=VMEM_LIMIT),
        name="inproj",
    )(x, g, wa, wb, wgate, cos, sin_signed)


def _log_sigmoid(x):
    return jnp.minimum(x, 0.0) - jnp.log(1.0 + jnp.exp(-jnp.abs(x)))


def _mlstm_body(qp_ref, kp_ref, v_ref, og_ref, gt_ref, bias_ref, cwq_ref, cwk_ref, gain_ref, out_ref,
                q_s, kt_s, rows_s, cm_s, bb_s, c_s, h_s, *, n_hp):
    S = qp_ref.shape[0]
    L = CHUNK
    nc = S // L
    d = HEAD_DIM

    def conv_silu(x, w):
        row = lax.broadcasted_iota(jnp.int32, x.shape, 0)
        prev = jnp.where(row == 0, 0.0, pltpu.roll(x, 1, 0))
        nxt = jnp.where(row == S - 1, 0.0, pltpu.roll(x, S - 1, 0))
        y = prev * w[0:1, :] + x * w[1:2, :] + nxt * w[2:3, :]
        return y * jax.nn.sigmoid(y)

    r_i = lax.broadcasted_iota(jnp.int32, (L, L), 0)
    c_i = lax.broadcasted_iota(jnp.int32, (L, L), 1)
    lower = c_i <= r_i
    upper = c_i >= r_i
    eye = c_i == r_i

    for hh in range(n_hp):
        cols = slice(hh * d, (hh + 1) * d)
        q_s[:, cols] = conv_silu(qp_ref[:, cols].astype(f32), cwq_ref[:, cols]).astype(bf16)
        k_all = conv_silu(kp_ref[:, cols].astype(f32), cwk_ref[:, cols]) * (d ** -0.5)
        for c in range(nc):
            kt_s[hh, c] = k_all[c * L:(c + 1) * L, :].T.astype(bf16)

        bias = bias_ref[hh]
        ig_f = gt_ref[hh, 0] + bias[0:1, :]
        lf_f = _log_sigmoid(gt_ref[hh, 1] + bias[1:2, :])
        ig_b = gt_ref[hh, 2] + bias[2:3, :]
        lf_b = _log_sigmoid(gt_ref[hh, 3] + bias[3:4, :])
        b_f = jnp.dot(lf_f, upper.astype(f32), preferred_element_type=f32, precision=lax.Precision.HIGHEST)
        b_b = jnp.dot(lf_b, lower.astype(f32), preferred_element_type=f32, precision=lax.Precision.HIGHEST)
        rows_s[hh, 0] = b_f
        rows_s[hh, 1] = ig_f - b_f
        rows_s[hh, 2] = b_b
        rows_s[hh, 3] = ig_b - b_b

    sides = ((0, lower, L - 1), (1, upper, 0))

    chains = [(hh, dr, visible, full_row) for hh in range(n_hp) for dr, visible, full_row in sides]
    chunk_at = lambda dr, t: (nc - 1 - t) if dr else t

    def prep(hh, dr, visible, c):
        b_row = rows_s[hh, 2 * dr, pl.ds(c, 1), :]
        u_row = rows_s[hh, 2 * dr + 1, pl.ds(c, 1), :]
        dm = jnp.where(visible, jnp.broadcast_to(u_row, (L, L)), -jnp.inf)
        cm = jnp.max(dm, axis=1, keepdims=True)
        b_col = jnp.sum(jnp.where(eye, jnp.broadcast_to(b_row, (L, L)), 0.0), axis=1, keepdims=True)
        return jnp.broadcast_to(cm, (L, LANES)), jnp.broadcast_to(b_col, (L, LANES))

    def store_prep(slot_rows, tiles):
        for (hh, dr, _, _), (cm, bb) in zip(chains, tiles):
            cm_s[hh, dr, slot_rows, :] = cm
            bb_s[hh, dr, slot_rows, :] = bb

    store_prep(pl.ds(0, L), [prep(hh, dr, visible, chunk_at(dr, 0)) for hh, dr, visible, _ in chains])

    c_s[...] = jnp.zeros_like(c_s)
    ones = jnp.ones((L, LANES), bf16)
    twice = lambda a: jnp.concatenate([a, a], axis=1)

    def step(t, ms):
        first, second, m_out = [], [], []
        slot = pl.multiple_of((t % 2) * L, L)
        for (hh, dr, visible, full_row), m in zip(chains, ms):
            cols = slice(hh * d, (hh + 1) * d)
            c = chunk_at(dr, t)
            r0 = pl.multiple_of(c * L, L)
            u_row = rows_s[hh, 2 * dr + 1, pl.ds(c, 1), :]
            big_m = jnp.maximum(m, cm_s[hh, dr, pl.ds(slot, L), :])
            dm = jnp.where(visible, jnp.broadcast_to(u_row, (L, L)), -jnp.inf)
            w = jnp.exp(dm - big_m)
            floor = jnp.exp(-(bb_s[hh, dr, pl.ds(slot, L), :] + big_m))
            w_inter = jnp.exp(m - big_m)
            m_full = big_m[full_row:full_row + 1, :]
            ws = jnp.exp(u_row - m_full)
            wc = jnp.exp(m - m_full)
            m_out.append(bb_s[hh, dr, pl.ds(slot + full_row, 1), :] + m_full)

            q = q_s[pl.ds(r0, L), cols]
            kt = kt_s[hh, c]
            v_aug = jnp.concatenate([v_ref[pl.ds(r0, L), cols], ones], axis=1)
            state = c_s[hh, dr]
            s = jnp.dot(q, kt, preferred_element_type=f32)
            inter = jnp.dot(q, state.astype(bf16), preferred_element_type=f32)
            upd = jnp.dot((kt.astype(f32) * ws).astype(bf16), v_aug, preferred_element_type=f32)
            first.append((s, w, v_aug, twice(w_inter) * inter, floor, twice(wc) * state + upd))
        for s, w, v_aug, inter, floor, state in first:
            tot = jnp.dot((s * w).astype(bf16), v_aug, preferred_element_type=f32) + inter
            second.append((tot[:, :d] / jnp.maximum(jnp.abs(tot[:, d:]), floor), state))
        t_next = jnp.minimum(t + 1, nc - 1)
        ahead = [prep(hh, dr, visible, chunk_at(dr, t_next)) for hh, dr, visible, _ in chains]
        for (hh, dr, _, _), (h, state) in zip(chains, second):
            h_s[dr, pl.ds(pl.multiple_of(chunk_at(dr, t) * L, L), L), hh * d:(hh + 1) * d] = h
            c_s[hh, dr] = state
        store_prep(pl.ds(pl.multiple_of(((t + 1) % 2) * L, L), L), ahead)
        return tuple(m_out)

    zero = jnp.zeros((1, LANES), f32)
    lax.fori_loop(0, nc, step, (zero,) * (2 * n_hp))

    for hh in range(n_hp):
        cols = slice(hh * d, (hh + 1) * d)
        hm = _rms(h_s[0, :, cols] + h_s[1, :, cols], gain_ref[:, cols])
        out_ref[:, cols] = (jax.nn.sigmoid(og_ref[:, cols].astype(f32)) * hm).astype(out_ref.dtype)


def _mlstm(z, gt, bias, conv_w, gain, *, n_seq, S, n_heads, n_hp):
    T = z.shape[0]
    d = HEAD_DIM
    nc = S // CHUNK
    G = n_heads // n_hp
    w = n_hp * d
    seq_col = lambda off: pl.BlockSpec((S, w), lambda b, g: (b, off + g))
    return pl.pallas_call(
        functools.partial(_mlstm_body, n_hp=n_hp),
        grid=(n_seq, G),
        in_specs=[
            seq_col(0), seq_col(G), seq_col(2 * G), seq_col(3 * G),
            pl.BlockSpec((None, n_hp, N_GATE_KINDS, nc, CHUNK), lambda b, g: (b, g, 0, 0, 0)),
            pl.BlockSpec((n_hp, N_GATE_KINDS, LANES), lambda b, g: (g, 0, 0)),
            pl.BlockSpec((CONV_W, w), lambda b, g: (0, g)),
            pl.BlockSpec((CONV_W, w), lambda b, g: (0, G + g)),
            pl.BlockSpec((1, w), lambda b, g: (0, g)),
        ],
        out_specs=pl.BlockSpec((S, w), lambda b, g: (b, g)),
        out_shape=jax.ShapeDtypeStruct((T, n_heads * d), bf16),
        scratch_shapes=[
            pltpu.VMEM((S, w), bf16),
            pltpu.VMEM((n_hp, nc, d, CHUNK), bf16),
            pltpu.VMEM((n_hp, N_GATE_KINDS, nc, CHUNK), f32),
            pltpu.VMEM((n_hp, 2, 2 * CHUNK, LANES), f32),
            pltpu.VMEM((n_hp, 2, 2 * CHUNK, LANES), f32),
            pltpu.VMEM((n_hp, 2, d, 2 * LANES), f32),
            pltpu.VMEM((2, S, w), f32),
        ],
        compiler_params=pltpu.CompilerParams(
            dimension_semantics=("parallel", "parallel"), vmem_limit_bytes=VMEM_LIMIT),
        name="mlstm",
    )(z, z, z, z, gt, bias, conv_w, conv_w, gain)


def _attn_body(q_ref, k_ref, v_ref, lq1_ref, lk1_ref, lq2_ref, lk2_ref, g_ref, out_ref, *, lambda_init, tk):
    d = HEAD_DIM
    S = k_ref.shape[0]
    nt = (((1,), (1,)), ((), ()))

    def component(q, col):
        m = l = acc = None
        for c in range(S // tk):
            rows = slice(c * tk, (c + 1) * tk)
            s = lax.dot_general(q, k_ref[rows, col:col + d], nt, preferred_element_type=f32)
            m_c = jnp.max(s, axis=1, keepdims=True)
            if c == 0:
                m = m_c
                p = jnp.exp2(s - m)
                l = jnp.sum(p, axis=1, keepdims=True)
                acc = jnp.dot(p.astype(bf16), v_ref[rows, :], preferred_element_type=f32)
            else:
                m_new = jnp.maximum(m, m_c)
                alpha = jnp.exp2(m - m_new)
                p = jnp.exp2(s - m_new)
                l = alpha * l + jnp.sum(p, axis=1, keepdims=True)
                acc = alpha * acc + jnp.dot(p.astype(bf16), v_ref[rows, :], preferred_element_type=f32)
                m = m_new
        return acc, l

    acc1, l1 = component(q_ref[:, :d], 0)
    acc2, l2 = component(q_ref[:, d:], d)
    lam = (jnp.exp(jnp.sum(lq1_ref[...] * lk1_ref[...], axis=1, keepdims=True))
           - jnp.exp(jnp.sum(lq2_ref[...] * lk2_ref[...], axis=1, keepdims=True)) + lambda_init)
    o = acc1 * (1.0 / l1) - acc2 * (lam / l2)
    o = _rms(o, g_ref[...]) * (1.0 - lambda_init)
    out_ref[...] = o.astype(out_ref.dtype)


def _attn(z, lq1, lk1, lq2, lk2, g, *, n_seq, S, n_heads, col0, lambda_init, tq, tk):
    T = z.shape[0]
    d = HEAD_DIM
    nq = S // tq
    A = n_heads
    row = lambda b, h, i: (0, 0)
    return pl.pallas_call(
        functools.partial(_attn_body, lambda_init=lambda_init, tk=tk),
        grid=(n_seq, A, nq),
        in_specs=[
            pl.BlockSpec((tq, 2 * d), lambda b, h, i: (b * nq + i, col0 + h)),
            pl.BlockSpec((S, 2 * d), lambda b, h, i: (b, col0 + A + h)),
            pl.BlockSpec((S, 2 * d), lambda b, h, i: (b, col0 + 2 * A + h)),
            pl.BlockSpec((1, d), row), pl.BlockSpec((1, d), row),
            pl.BlockSpec((1, d), row), pl.BlockSpec((1, d), row),
            pl.BlockSpec((1, 2 * d), row),
        ],
        out_specs=pl.BlockSpec((tq, 2 * d), lambda b, h, i: (b * nq + i, h)),
        out_shape=jax.ShapeDtypeStruct((T, A * 2 * d), bf16),
        compiler_params=pltpu.CompilerParams(
            dimension_semantics=("parallel", "parallel", "parallel"), vmem_limit_bytes=VMEM_LIMIT),
        name="diffattn",
    )(z, z, z, lq1, lk1, lq2, lk2, g)


def _outproj_body(x_ref, om_ref, oa_ref, wm_ref, wa_ref, o_ref):
    o_ref[...] = (x_ref[...] + jnp.dot(om_ref[...], wm_ref[...], preferred_element_type=f32)
                  + jnp.dot(oa_ref[...], wa_ref[...], preferred_element_type=f32))


def _outproj(x, om, oa, w, *, layer, tm):
    T, D = x.shape
    Wm, Wa = om.shape[1], oa.shape[1]
    assert Wm == Wa and Wm + Wa == D
    return pl.pallas_call(
        _outproj_body,
        grid=(T // tm,),
        in_specs=[
            pl.BlockSpec((tm, D), lambda i: (i, 0)),
            pl.BlockSpec((tm, Wm), lambda i: (i, 0)),
            pl.BlockSpec((tm, Wa), lambda i: (i, 0)),
            pl.BlockSpec((None, Wm, D), lambda i: (layer, 0, 0)),
            pl.BlockSpec((None, Wa, D), lambda i: (layer, 1, 0)),
        ],
        out_specs=pl.BlockSpec((tm, D), lambda i: (i, 0)),
        out_shape=jax.ShapeDtypeStruct((T, D), f32),
        compiler_params=pltpu.CompilerParams(
            dimension_semantics=("parallel",), vmem_limit_bytes=VMEM_LIMIT),
        name="outproj",
    )(x, om, oa, w, w)


def _tiles(T, S):
    g = math.gcd(T, S)
    return math.gcd(512, g), math.gcd(512, g), math.gcd(1024, g), math.gcd(512, S), math.gcd(1024, S)


def kernel(x_prompt, x_sample, ffn1_norm, ffn1_w_gate, ffn1_w_up, ffn1_w_down, mix_norm, w_in, conv_qk, b_gate,
           mlstm_norm, lambda_q1, lambda_k1, lambda_q2, lambda_k2, diff_subln, w_out, ffn2_norm, ffn2_w_gate,
           ffn2_w_up, ffn2_w_down, final_norm):
    depth, D, F = ffn1_w_gate.shape
    S = x_prompt.shape[1]
    assert x_sample.shape[1] == S and S % CHUNK == 0
    d = HEAD_DIM
    m_w = D // 2
    MH = m_w // d
    a_w = D - m_w
    AH = a_w // (2 * d)
    n_gates = N_GATE_KINDS * MH
    assert n_gates <= LANES and w_in.shape[2] == 4 * m_w + n_gates + 3 * a_w
    off_g = 4 * m_w
    n_seq = x_prompt.shape[0] + x_sample.shape[0]
    T = n_seq * S
    tm, tm_ffn, tm_in, tq, tk = _tiles(T, S)
    tf = min(512, _round_up(F, LANES))
    tn = math.gcd(1024, math.gcd(4 * m_w, a_w))
    nc = S // CHUNK
    n_p = x_prompt.shape[0] * S

    inv = 1.0 / (ROPE_THETA ** (jnp.arange(0, d, 2, dtype=f32) / d))
    ang = jnp.arange(S, dtype=f32)[:, None] * inv[None, :]
    emb = jnp.concatenate([ang, ang], axis=-1)
    cos = jnp.cos(emb)
    sin_signed = jnp.sin(emb) * jnp.concatenate([-jnp.ones((d // 2,), f32), jnp.ones((d // 2,), f32)])

    ffn_w = [tuple(w.astype(bf16) for w in ws) for ws in
             ((ffn1_w_gate, ffn1_w_up, ffn1_w_down), (ffn2_w_gate, ffn2_w_up, ffn2_w_down))]
    ffn_g = (ffn1_norm, ffn2_norm)
    w_a = w_in[:, :, :off_g].astype(bf16)
    w_b = w_in[:, :, off_g + n_gates:].astype(bf16)
    w_gate = w_in[:, :, off_g:off_g + n_gates].reshape(depth, D, N_GATE_KINDS, MH).transpose(0, 1, 3, 2)
    w_gate = jnp.pad(w_gate.reshape(depth, D, n_gates), ((0, 0), (0, 0), (0, LANES - n_gates))).astype(bf16)
    w_o = w_out.astype(bf16)

    row = lambda v: v.reshape(1, -1).astype(f32)

    def ffn(xs, which, l, out_rows):
        wg, wu, wd = ffn_w[which]
        return _ffn(xs, row(ffn_g[which][l]), wg, wu, wd, row(final_norm), layer=l,
                    final_norm=(which == 1 and l == depth - 1), tm=tm_ffn, tf=tf, out_rows=out_rows)

    xs = (x_prompt.reshape(-1, D), x_sample.reshape(-1, D))
    for l in range(depth):
        lambda_init = 0.8 - 0.6 * math.exp(-0.3 * l)
        (x,) = ffn(xs, 0, l, (T,))
        z, gates = _inproj(x, row(mix_norm[l]), w_a, w_b, w_gate, cos, sin_signed, layer=l, tm=tm_in, tn=tn, S=S,
                           q_cols=(4 * m_w, 4 * m_w + a_w), k_cols=(4 * m_w + a_w, 4 * m_w + 2 * a_w),
                           q_scale=d ** -0.5 * math.log2(math.e))

        gt = gates[:, :n_gates].reshape(n_seq, nc, CHUNK, MH, N_GATE_KINDS).transpose(0, 3, 4, 1, 2)
        bias = jnp.broadcast_to(b_gate[l].astype(f32).T[:, :, None], (MH, N_GATE_KINDS, LANES))
        om = _mlstm(z, gt, bias, conv_qk[l].astype(f32), row(mlstm_norm[l]), n_seq=n_seq, S=S, n_heads=MH,
                    n_hp=math.gcd(2, MH))
        oa = _attn(z, row(lambda_q1[l]), row(lambda_k1[l]), row(lambda_q2[l]), row(lambda_k2[l]),
                   row(diff_subln[l]), n_seq=n_seq, S=S, n_heads=AH, col0=4 * m_w // (2 * d),
                   lambda_init=lambda_init, tq=tq, tk=tk)
        x = _outproj(x, om, oa, w_o, layer=l, tm=tm)
        xs = ffn((x,), 1, l, (n_p, T - n_p) if l == depth - 1 else (T,))

    return xs[0].reshape(x_prompt.shape), xs[1].reshape(x_sample.shape)
```

```python
import functools
import math

import jax
import jax.numpy as jnp
from jax import lax
from jax.experimental import pallas as pl
from jax.experimental.pallas import tpu as pltpu

EPS = 1e-6
HEAD_DIM = 128
CHUNK = 128
CONV_W = 3
ROPE_THETA = 10000.0
N_GATE_KINDS = 4
LANES = 128
VMEM_LIMIT = 60 * 1024 * 1024

f32 = jnp.float32
bf16 = jnp.bfloat16


def _round_up(n, m):
    return (n + m - 1) // m * m


def _rms(x, g):
    ms = jnp.mean(x * x, axis=-1, keepdims=True)
    return x * lax.rsqrt(ms + EPS) * g


def _part_tiles(rows, tm):
    tiles, lo = [], 0
    for r in rows:
        assert r % tm == 0
        tiles.append((lo, lo + r // tm))
        lo += r // tm
    return tuple(tiles)


def _ffn_body(*refs, in_tiles, out_tiles, final_norm, f_last):
    n_in, n_out = len(in_tiles), len(out_tiles)
    x_refs = refs[:n_in]
    g_ref, wg_ref, wu_ref, wd_ref, fg_ref = refs[n_in:n_in + 5]
    o_refs = refs[n_in + 5:n_in + 5 + n_out]
    xn_ref = refs[n_in + 5 + n_out]
    acc_ref = o_refs[0] if n_out == 1 else refs[n_in + 6 + n_out]
    i = pl.program_id(0)
    j = pl.program_id(1)
    last = pl.num_programs(1) - 1
    tf = wg_ref.shape[1]

    def in_part(tiles, p, fn):
        if len(tiles) == 1:
            fn()
        else:
            pl.when((i >= tiles[p][0]) & (i < tiles[p][1]))(fn)

    @pl.when(j == 0)
    def _():
        for p in range(n_in):
            def start(x_ref=x_refs[p]):
                x = x_ref[...]
                xn_ref[...] = _rms(x, g_ref[...]).astype(bf16)
                acc_ref[...] = x
            in_part(in_tiles, p, start)

    def accumulate(nv):
        xn = xn_ref[...]
        gate = jnp.dot(xn, wg_ref[:, :nv], preferred_element_type=f32)
        up = jnp.dot(xn, wu_ref[:, :nv], preferred_element_type=f32)
        h = (gate * jax.nn.sigmoid(gate) * up * 0.5).astype(bf16)
        acc_ref[...] += jnp.dot(h, wd_ref[:nv, :], preferred_element_type=f32)

    if f_last == tf:
        accumulate(tf)
    else:
        pl.when(j < last)(functools.partial(accumulate, tf))
        pl.when(j == last)(functools.partial(accumulate, f_last))

    if final_norm or n_out > 1:
        @pl.when(j == last)
        def _():
            for p in range(n_out):
                def finish(o_ref=o_refs[p]):
                    y = acc_ref[...]
                    o_ref[...] = _rms(y, fg_ref[...]) if final_norm else y
                in_part(out_tiles, p, finish)


def _ffn(xs, g, wg, wu, wd, fg, *, layer, final_norm, tm, tf, out_rows):
    D = xs[0].shape[1]
    T = sum(x.shape[0] for x in xs)
    F = wd.shape[1]
    n_f = wg.shape[1]
    assert wg.shape[3] == tf and n_f == pl.cdiv(F, tf)
    in_tiles = _part_tiles([x.shape[0] for x in xs], tm)
    out_tiles = _part_tiles(out_rows, tm)
    assert sum(out_rows) == T

    def part_spec(lo, hi):
        return pl.BlockSpec((tm, D), lambda i, j: (jnp.clip(i - lo, 0, hi - lo - 1), 0))

    outs = pl.pallas_call(
        functools.partial(_ffn_body, in_tiles=in_tiles, out_tiles=out_tiles, final_norm=final_norm,
                          f_last=F - (n_f - 1) * tf),
        grid=(T // tm, n_f),
        in_specs=[part_spec(lo, hi) for lo, hi in in_tiles] + [
            pl.BlockSpec((1, D), lambda i, j: (0, 0)),
            pl.BlockSpec((None, None, D, tf), lambda i, j: (layer, j, 0, 0)),
            pl.BlockSpec((None, None, D, tf), lambda i, j: (layer, j, 0, 0)),
            pl.BlockSpec((None, tf, D), lambda i, j: (layer, j, 0)),
            pl.BlockSpec((1, D), lambda i, j: (0, 0)),
        ],
        out_specs=[part_spec(lo, hi) for lo, hi in out_tiles],
        out_shape=[jax.ShapeDtypeStruct((r, D), f32) for r in out_rows],
        scratch_shapes=[pltpu.VMEM((tm, D), bf16)] + ([pltpu.VMEM((tm, D), f32)] if len(out_rows) > 1 else []),
        compiler_params=pltpu.CompilerParams(
            dimension_semantics=("arbitrary", "arbitrary"), vmem_limit_bytes=VMEM_LIMIT),
        name="ffn",
    )(*xs, g, wg, wu, wd, fg)
    return outs


def _rope(x, cos, sin_signed):
    return x * cos + pltpu.roll(x, HEAD_DIM // 2, 1) * sin_signed


def _inproj_body(x_ref, g_ref, wa_ref, wb_ref, wgate_ref, cos_ref, sin_ref, z_ref, gates_ref, xn_ref, *, n_a,
                 q_tiles, k_tiles, q_scale):
    j = pl.program_id(1)

    @pl.when(j == 0)
    def _():
        xn = _rms(x_ref[...], g_ref[...]).astype(bf16)
        xn_ref[...] = xn
        gates_ref[...] = jnp.dot(xn, wgate_ref[...], preferred_element_type=f32)

    is_q = (j >= q_tiles[0]) & (j < q_tiles[1])
    is_k = (j >= k_tiles[0]) & (j < k_tiles[1])
    rotary = is_q | is_k

    @pl.when(j < n_a)
    def _():
        z_ref[...] = jnp.dot(xn_ref[...], wa_ref[...], preferred_element_type=f32).astype(z_ref.dtype)

    @pl.when((j >= n_a) & jnp.logical_not(rotary))
    def _():
        z_ref[...] = jnp.dot(xn_ref[...], wb_ref[...], preferred_element_type=f32).astype(z_ref.dtype)

    @pl.when(rotary)
    def _():
        acc = jnp.dot(xn_ref[...], wb_ref[...], preferred_element_type=f32)
        sc = jnp.where(is_q, q_scale, 1.0)
        cos = cos_ref[...] * sc
        sin = sin_ref[...] * sc
        for c in range(acc.shape[1] // HEAD_DIM):
            cols = slice(c * HEAD_DIM, (c + 1) * HEAD_DIM)
            z_ref[:, cols] = _rope(acc[:, cols], cos, sin).astype(z_ref.dtype)


def _inproj(x, g, wa, wb, wgate, cos, sin_signed, *, layer, tm, tn, S, q_cols, k_cols, q_scale):
    T, D = x.shape
    n_a = wa.shape[1]
    Na, Nb = n_a * tn, wb.shape[1] * tn
    assert wa.shape[3] == tn and wb.shape[3] == tn
    n_pos = S // tm
    tiles = lambda cols: (cols[0] // tn, cols[1] // tn)
    assert all(c % tn == 0 for c in q_cols + k_cols + (Na, Nb)) and q_cols[0] >= Na and k_cols[0] >= Na and S % tm == 0
    return pl.pallas_call(
        functools.partial(_inproj_body, n_a=n_a, q_tiles=tiles(q_cols), k_tiles=tiles(k_cols), q_scale=q_scale),
        grid=(T // tm, (Na + Nb) // tn),
        in_specs=[
            pl.BlockSpec((tm, D), lambda i, j: (i, 0)),
            pl.BlockSpec((1, D), lambda i, j: (0, 0)),
            pl.BlockSpec((None, None, D, tn), lambda i, j: (layer, jnp.minimum(j, n_a - 1), 0, 0)),
            pl.BlockSpec((None, None, D, tn), lambda i, j: (layer, jnp.maximum(j - n_a, 0), 0, 0)),
            pl.BlockSpec((None, D, LANES), lambda i, j: (layer, 0, 0)),
            pl.BlockSpec((tm, HEAD_DIM), lambda i, j: (i % n_pos, 0)),
            pl.BlockSpec((tm, HEAD_DIM), lambda i, j: (i % n_pos, 0)),
        ],
        out_specs=[
            pl.BlockSpec((tm, tn), lambda i, j: (i, j)),
            pl.BlockSpec((tm, LANES), lambda i, j: (i, 0)),
        ],
        out_shape=[jax.ShapeDtypeStruct((T, Na + Nb), bf16), jax.ShapeDtypeStruct((T, LANES), f32)],
        scratch_shapes=[pltpu.VMEM((tm, D), bf16)],
        compiler_params=pltpu.CompilerParams(
            dimension_semantics=("parallel", "arbitrary"), vmem_limit_bytes=VMEM_LIMIT),
        name="inproj",
    )(x, g, wa, wb, wgate, cos, sin_signed)


def _log_sigmoid(x):
    return jnp.minimum(x, 0.0) - jnp.log(1.0 + jnp.exp(-jnp.abs(x)))


def _mlstm_body(qp_ref, kp_ref, v_ref, og_ref, gt_ref, bias_ref, cwq_ref, cwk_ref, gain_ref, out_ref,
                q_s, kt_s, rows_s, cm_s, bb_s, c_s, h_s, *, n_hp):
    S = qp_ref.shape[0]
    L = CHUNK
    nc = S // L
    d = HEAD_DIM

    def conv_silu(x, w):
        row = lax.broadcasted_iota(jnp.int32, x.shape, 0)
        prev = jnp.where(row == 0, 0.0, pltpu.roll(x, 1, 0))
        nxt = jnp.where(row == S - 1, 0.0, pltpu.roll(x, S - 1, 0))
        y = prev * w[0:1, :] + x * w[1:2, :] + nxt * w[2:3, :]
        return y * jax.nn.sigmoid(y)

    r_i = lax.broadcasted_iota(jnp.int32, (L, L), 0)
    c_i = lax.broadcasted_iota(jnp.int32, (L, L), 1)
    lower = c_i <= r_i
    upper = c_i >= r_i
    eye = c_i == r_i

    for hh in range(n_hp):
        cols = slice(hh * d, (hh + 1) * d)
        q_s[:, cols] = conv_silu(qp_ref[:, cols].astype(f32), cwq_ref[:, cols]).astype(bf16)
        k_all = conv_silu(kp_ref[:, cols].astype(f32), cwk_ref[:, cols]) * (d ** -0.5)
        for c in range(nc):
            kt_s[hh, c] = k_all[c * L:(c + 1) * L, :].T.astype(bf16)

        bias = bias_ref[hh]
        ig_f = gt_ref[hh, 0] + bias[0:1, :]
        lf_f = _log_sigmoid(gt_ref[hh, 1] + bias[1:2, :])
        ig_b = gt_ref[hh, 2] + bias[2:3, :]
        lf_b = _log_sigmoid(gt_ref[hh, 3] + bias[3:4, :])
        b_f = jnp.dot(lf_f, upper.astype(f32), preferred_element_type=f32, precision=lax.Precision.HIGHEST)
        b_b = jnp.dot(lf_b, lower.astype(f32), preferred_element_type=f32, precision=lax.Precision.HIGHEST)
        rows_s[hh, 0] = b_f
        rows_s[hh, 1] = ig_f - b_f
        rows_s[hh, 2] = b_b
        rows_s[hh, 3] = ig_b - b_b

    sides = ((0, lower, L - 1), (1, upper, 0))

    chains = [(hh, dr, visible, full_row) for hh in range(n_hp) for dr, visible, full_row in sides]
    chunk_at = lambda dr, t: (nc - 1 - t) if dr else t

    def prep(hh, dr, visible, c):
        b_row = rows_s[hh, 2 * dr, pl.ds(c, 1), :]
        u_row = rows_s[hh, 2 * dr + 1, pl.ds(c, 1), :]
        dm = jnp.where(visible, jnp.broadcast_to(u_row, (L, L)), -jnp.inf)
        cm = jnp.max(dm, axis=1, keepdims=True)
        b_col = jnp.sum(jnp.where(eye, jnp.broadcast_to(b_row, (L, L)), 0.0), axis=1, keepdims=True)
        return jnp.broadcast_to(cm, (L, LANES)), jnp.broadcast_to(b_col, (L, LANES))

    def store_prep(slot_rows, tiles):
        for (hh, dr, _, _), (cm, bb) in zip(chains, tiles):
            cm_s[hh, dr, slot_rows, :] = cm
            bb_s[hh, dr, slot_rows, :] = bb

    store_prep(pl.ds(0, L), [prep(hh, dr, visible, chunk_at(dr, 0)) for hh, dr, visible, _ in chains])

    c_s[...] = jnp.zeros_like(c_s)
    ones = jnp.ones((L, LANES), bf16)
    twice = lambda a: jnp.concatenate([a, a], axis=1)

    def step(t, ms):
        first, second, m_out = [], [], []
        slot = pl.multiple_of((t % 2) * L, L)
        for (hh, dr, visible, full_row), m in zip(chains, ms):
            cols = slice(hh * d, (hh + 1) * d)
            c = chunk_at(dr, t)
            r0 = pl.multiple_of(c * L, L)
            u_row = rows_s[hh, 2 * dr + 1, pl.ds(c, 1), :]
            big_m = jnp.maximum(m, cm_s[hh, dr, pl.ds(slot, L), :])
            dm = jnp.where(visible, jnp.broadcast_to(u_row, (L, L)), -jnp.inf)
            w = jnp.exp(dm - big_m)
            floor = jnp.exp(-(bb_s[hh, dr, pl.ds(slot, L), :] + big_m))
            w_inter = jnp.exp(m - big_m)
            m_full = big_m[full_row:full_row + 1, :]
            ws = jnp.exp(u_row - m_full)
            wc = jnp.exp(m - m_full)
            m_out.append(bb_s[hh, dr, pl.ds(slot + full_row, 1), :] + m_full)

            q = q_s[pl.ds(r0, L), cols]
            kt = kt_s[hh, c]
            v_aug = jnp.concatenate([v_ref[pl.ds(r0, L), cols], ones], axis=1)
            state = c_s[hh, dr]
            s = jnp.dot(q, kt, preferred_element_type=f32)
            inter = jnp.dot(q, state.astype(bf16), preferred_element_type=f32)
            upd = jnp.dot((kt.astype(f32) * ws).astype(bf16), v_aug, preferred_element_type=f32)
            first.append((s, w, v_aug, twice(w_inter) * inter, floor, twice(wc) * state + upd))
        for s, w, v_aug, inter, floor, state in first:
            tot = jnp.dot((s * w).astype(bf16), v_aug, preferred_element_type=f32) + inter
            second.append((tot[:, :d] / jnp.maximum(jnp.abs(tot[:, d:]), floor), state))
        t_next = jnp.minimum(t + 1, nc - 1)
        ahead = [prep(hh, dr, visible, chunk_at(dr, t_next)) for hh, dr, visible, _ in chains]
        for (hh, dr, _, _), (h, state) in zip(chains, second):
            h_s[dr, pl.ds(pl.multiple_of(chunk_at(dr, t) * L, L), L), hh * d:(hh + 1) * d] = h
            c_s[hh, dr] = state
        store_prep(pl.ds(pl.multiple_of(((t + 1) % 2) * L, L), L), ahead)
        return tuple(m_out)

    zero = jnp.zeros((1, LANES), f32)
    lax.fori_loop(0, nc, step, (zero,) * (2 * n_hp))

    for hh in range(n_hp):
        cols = slice(hh * d, (hh + 1) * d)
        hm = _rms(h_s[0, :, cols] + h_s[1, :, cols], gain_ref[:, cols])
        out_ref[:, cols] = (jax.nn.sigmoid(og_ref[:, cols].astype(f32)) * hm).astype(out_ref.dtype)


def _mlstm(z, gt, bias, conv_w, gain, *, n_seq, S, n_heads, n_hp):
    T = z.shape[0]
    d = HEAD_DIM
    nc = S // CHUNK
    G = n_heads // n_hp
    w = n_hp * d
    seq_col = lambda off: pl.BlockSpec((S, w), lambda b, g: (b, off + g))
    return pl.pallas_call(
        functools.partial(_mlstm_body, n_hp=n_hp),
        grid=(n_seq, G),
        in_specs=[
            seq_col(0), seq_col(G), seq_col(2 * G), seq_col(3 * G),
            pl.BlockSpec((None, n_hp, N_GATE_KINDS, nc, CHUNK), lambda b, g: (b, g, 0, 0, 0)),
            pl.BlockSpec((n_hp, N_GATE_KINDS, LANES), lambda b, g: (g, 0, 0)),
            pl.BlockSpec((CONV_W, w), lambda b, g: (0, g)),
            pl.BlockSpec((CONV_W, w), lambda b, g: (0, G + g)),
            pl.BlockSpec((1, w), lambda b, g: (0, g)),
        ],
        out_specs=pl.BlockSpec((S, w), lambda b, g: (b, g)),
        out_shape=jax.ShapeDtypeStruct((T, n_heads * d), bf16),
        scratch_shapes=[
            pltpu.VMEM((S, w), bf16),
            pltpu.VMEM((n_hp, nc, d, CHUNK), bf16),
            pltpu.VMEM((n_hp, N_GATE_KINDS, nc, CHUNK), f32),
            pltpu.VMEM((n_hp, 2, 2 * CHUNK, LANES), f32),
            pltpu.VMEM((n_hp, 2, 2 * CHUNK, LANES), f32),
            pltpu.VMEM((n_hp, 2, d, 2 * LANES), f32),
            pltpu.VMEM((2, S, w), f32),
        ],
        compiler_params=pltpu.CompilerParams(
            dimension_semantics=("parallel", "parallel"), vmem_limit_bytes=VMEM_LIMIT),
        name="mlstm",
    )(z, z, z, z, gt, bias, conv_w, conv_w, gain)


def _attn_body(q_ref, k_ref, v_ref, lq1_ref, lk1_ref, lq2_ref, lk2_ref, g_ref, out_ref, *, lambda_init, tk):
    d = HEAD_DIM
    S = k_ref.shape[0]
    nt = (((1,), (1,)), ((), ()))

    def component(q, col):
        m = l = acc = None
        for c in range(S // tk):
            rows = slice(c * tk, (c + 1) * tk)
            s = lax.dot_general(q, k_ref[rows, col:col + d], nt, preferred_element_type=f32)
            m_c = jnp.max(s, axis=1, keepdims=True)
            if c == 0:
                m = m_c
                p = jnp.exp2(s - m)
                l = jnp.sum(p, axis=1, keepdims=True)
                acc = jnp.dot(p.astype(bf16), v_ref[rows, :], preferred_element_type=f32)
            else:
                m_new = jnp.maximum(m, m_c)
                alpha = jnp.exp2(m - m_new)
                p = jnp.exp2(s - m_new)
                l = alpha * l + jnp.sum(p, axis=1, keepdims=True)
                acc = alpha * acc + jnp.dot(p.astype(bf16), v_ref[rows, :], preferred_element_type=f32)
                m = m_new
        return acc, l

    acc1, l1 = component(q_ref[:, :d], 0)
    acc2, l2 = component(q_ref[:, d:], d)
    lam = (jnp.exp(jnp.sum(lq1_ref[...] * lk1_ref[...], axis=1, keepdims=True))
           - jnp.exp(jnp.sum(lq2_ref[...] * lk2_ref[...], axis=1, keepdims=True)) + lambda_init)
    o = acc1 * (1.0 / l1) - acc2 * (lam / l2)
    o = _rms(o, g_ref[...]) * (1.0 - lambda_init)
    out_ref[...] = o.astype(out_ref.dtype)


def _attn(z, lq1, lk1, lq2, lk2, g, *, n_seq, S, n_heads, col0, lambda_init, tq, tk):
    T = z.shape[0]
    d = HEAD_DIM
    nq = S // tq
    A = n_heads
    row = lambda b, h, i: (0, 0)
    return pl.pallas_call(
        functools.partial(_attn_body, lambda_init=lambda_init, tk=tk),
        grid=(n_seq, A, nq),
        in_specs=[
            pl.BlockSpec((tq, 2 * d), lambda b, h, i: (b * nq + i, col0 + h)),
            pl.BlockSpec((S, 2 * d), lambda b, h, i: (b, col0 + A + h)),
            pl.BlockSpec((S, 2 * d), lambda b, h, i: (b, col0 + 2 * A + h)),
            pl.BlockSpec((1, d), row), pl.BlockSpec((1, d), row),
            pl.BlockSpec((1, d), row), pl.BlockSpec((1, d), row),
            pl.BlockSpec((1, 2 * d), row),
        ],
        out_specs=pl.BlockSpec((tq, 2 * d), lambda b, h, i: (b * nq + i, h)),
        out_shape=jax.ShapeDtypeStruct((T, A * 2 * d), bf16),
        compiler_params=pltpu.CompilerParams(
            dimension_semantics=("parallel", "parallel", "parallel"), vmem_limit_bytes=VMEM_LIMIT),
        name="diffattn",
    )(z, z, z, lq1, lk1, lq2, lk2, g)


def _outproj_body(x_ref, om_ref, oa_ref, wm_ref, wa_ref, o_ref):
    o_ref[...] = (x_ref[...] + jnp.dot(om_ref[...], wm_ref[...], preferred_element_type=f32)
                  + jnp.dot(oa_ref[...], wa_ref[...], preferred_element_type=f32))


def _outproj(x, om, oa, w, *, layer, tm):
    T, D = x.shape
    Wm, Wa = om.shape[1], oa.shape[1]
    assert Wm == Wa and Wm + Wa == D
    return pl.pallas_call(
        _outproj_body,
        grid=(T // tm,),
        in_specs=[
            pl.BlockSpec((tm, D), lambda i: (i, 0)),
            pl.BlockSpec((tm, Wm), lambda i: (i, 0)),
            pl.BlockSpec((tm, Wa), lambda i: (i, 0)),
            pl.BlockSpec((None, Wm, D), lambda i: (layer, 0, 0)),
            pl.BlockSpec((None, Wa, D), lambda i: (layer, 1, 0)),
        ],
        out_specs=pl.BlockSpec((tm, D), lambda i: (i, 0)),
        out_shape=jax.ShapeDtypeStruct((T, D), f32),
        compiler_params=pltpu.CompilerParams(
            dimension_semantics=("parallel",), vmem_limit_bytes=VMEM_LIMIT),
        name="outproj",
    )(x, om, oa, w, w)


def _tiles(T, S):
    g = math.gcd(T, S)
    return math.gcd(512, g), math.gcd(512, g), math.gcd(1024, g), math.gcd(512, S), math.gcd(1024, S)


def kernel(x_prompt, x_sample, ffn1_norm, ffn1_w_gate, ffn1_w_up, ffn1_w_down, mix_norm, w_in, conv_qk, b_gate,
           mlstm_norm, lambda_q1, lambda_k1, lambda_q2, lambda_k2, diff_subln, w_out, ffn2_norm, ffn2_w_gate,
           ffn2_w_up, ffn2_w_down, final_norm):
    depth, D, F = ffn1_w_gate.shape
    S = x_prompt.shape[1]
    assert x_sample.shape[1] == S and S % CHUNK == 0
    d = HEAD_DIM
    m_w = D // 2
    MH = m_w // d
    a_w = D - m_w
    AH = a_w // (2 * d)
    n_gates = N_GATE_KINDS * MH
    assert n_gates <= LANES and w_in.shape[2] == 4 * m_w + n_gates + 3 * a_w
    off_g = 4 * m_w
    n_seq = x_prompt.shape[0] + x_sample.shape[0]
    T = n_seq * S
    tm, tm_ffn, tm_in, tq, tk = _tiles(T, S)
    tf = min(512, _round_up(F, LANES))
    tn = math.gcd(1024, math.gcd(4 * m_w, a_w))
    nc = S // CHUNK
    n_p = x_prompt.shape[0] * S

    inv = 1.0 / (ROPE_THETA ** (jnp.arange(0, d, 2, dtype=f32) / d))
    ang = jnp.arange(S, dtype=f32)[:, None] * inv[None, :]
    emb = jnp.concatenate([ang, ang], axis=-1)
    cos = jnp.cos(emb)
    sin_signed = jnp.sin(emb) * jnp.concatenate([-jnp.ones((d // 2,), f32), jnp.ones((d // 2,), f32)])

    def col_tiles(w, t):
        n = pl.cdiv(w.shape[2], t)
        w = jnp.pad(w.astype(bf16), ((0, 0), (0, 0), (0, n * t - w.shape[2])))
        return w.reshape(w.shape[0], w.shape[1], n, t).transpose(0, 2, 1, 3)

    ffn_w = [(col_tiles(g, tf), col_tiles(u, tf), dn.astype(bf16)) for g, u, dn in
             ((ffn1_w_gate, ffn1_w_up, ffn1_w_down), (ffn2_w_gate, ffn2_w_up, ffn2_w_down))]
    ffn_g = (ffn1_norm, ffn2_norm)
    w_a = col_tiles(w_in[:, :, :off_g], tn)
    w_b = col_tiles(w_in[:, :, off_g + n_gates:], tn)
    w_gate = w_in[:, :, off_g:off_g + n_gates].reshape(depth, D, N_GATE_KINDS, MH).transpose(0, 1, 3, 2)
    w_gate = jnp.pad(w_gate.reshape(depth, D, n_gates), ((0, 0), (0, 0), (0, LANES - n_gates))).astype(bf16)
    w_o = w_out.astype(bf16)

    row = lambda v: v.reshape(1, -1).astype(f32)

    def ffn(xs, which, l, out_rows):
        wg, wu, wd = ffn_w[which]
        return _ffn(xs, row(ffn_g[which][l]), wg, wu, wd, row(final_norm), layer=l,
                    final_norm=(which == 1 and l == depth - 1), tm=tm_ffn, tf=tf, out_rows=out_rows)

    xs = (x_prompt.reshape(-1, D), x_sample.reshape(-1, D))
    for l in range(depth):
        lambda_init = 0.8 - 0.6 * math.exp(-0.3 * l)
        (x,) = ffn(xs, 0, l, (T,))
        z, gates = _inproj(x, row(mix_norm[l]), w_a, w_b, w_gate, cos, sin_signed, layer=l, tm=tm_in, tn=tn, S=S,
                           q_cols=(4 * m_w, 4 * m_w + a_w), k_cols=(4 * m_w + a_w, 4 * m_w + 2 * a_w),
                           q_scale=d ** -0.5 * math.log2(math.e))

        gt = gates[:, :n_gates].reshape(n_seq, nc, CHUNK, MH, N_GATE_KINDS).transpose(0, 3, 4, 1, 2)
        bias = jnp.broadcast_to(b_gate[l].astype(f32).T[:, :, None], (MH, N_GATE_KINDS, LANES))
        om = _mlstm(z, gt, bias, conv_qk[l].astype(f32), row(mlstm_norm[l]), n_seq=n_seq, S=S, n_heads=MH,
                    n_hp=math.gcd(2, MH))
        oa = _attn(z, row(lambda_q1[l]), row(lambda_k1[l]), row(lambda_q2[l]), row(lambda_k2[l]),
                   row(diff_subln[l]), n_seq=n_seq, S=S, n_heads=AH, col0=4 * m_w // (2 * d),
                   lambda_init=lambda_init, tq=tq, tk=tk)
        x = _outproj(x, om, oa, w_o, layer=l, tm=tm)
        xs = ffn((x,), 1, l, (n_p, T - n_p) if l == depth - 1 else (T,))

    return xs[0].reshape(x_prompt.shape), xs[1].reshape(x_sample.shape)
```
